```python
import math
import jax, jax.numpy as jnp
from jax import lax
import numpy as np

D_MODEL = 4096
BATCH = 1
SEQ = 8192
DEPTH = 2
DEC_BATCH = 2
DEC_SEQ = 8192
PAST_LEN = 128

MLA_HEADS = 16
MLA_NOPE = 128
MLA_ROPE = 64
MLA_QK = MLA_NOPE + MLA_ROPE
MLA_V = 128
Q_LORA = 768
KV_LORA = 512
MLA_ROPE_THETA = 10000.0
DIFF_HEADS = 8
DIFF_HD = 128
DIFF_V = 2 * DIFF_HD
ROT_DIM = DIFF_HD // 4
ROPE_THETA = 500000.0
MIX_WIDTH = MLA_HEADS * MLA_V + DIFF_HEADS * DIFF_V
IN_SPLITS = (Q_LORA, KV_LORA, MLA_ROPE, DIFF_HEADS * 2 * DIFF_HD, DIFF_HEADS * 2 * DIFF_HD, DIFF_HEADS * DIFF_V)
IN_WIDTH = Q_LORA + KV_LORA + MLA_ROPE + 2 * (DIFF_HEADS * 2 * DIFF_HD) + DIFF_HEADS * DIFF_V
PEER_HEADS = 8
N_KEYS = 128
N_EXPERTS = N_KEYS * N_KEYS
PEER_TOPK = 16
PEER_DK = 256
PEER_HALF = PEER_DK // 2
Q_BLOCK = 128
TOK_BLOCK = 128
EPS = 1e-6

kernel_name = "hymba_mla_diffattn_peer_encoder"


def _rms_norm(x, g):
    xf = x.astype(jnp.float32)
    y = xf * lax.rsqrt(jnp.mean(xf * xf, axis=-1, keepdims=True) + EPS)
    return (y * g.astype(jnp.float32)).astype(x.dtype)


def _rope(x, theta):
    s, d = x.shape[1], x.shape[-1]
    inv = theta ** (-jnp.arange(0, d, 2, dtype=jnp.float32) / d)
    ang = jnp.arange(s, dtype=jnp.float32)[:, None] * inv[None, :]
    cos = jnp.cos(ang)[None, :, None, :]
    sin = jnp.sin(ang)[None, :, None, :]
    xf = x.astype(jnp.float32)
    x1, x2 = xf[..., : d // 2], xf[..., d // 2:]
    return jnp.concatenate([x1 * cos - x2 * sin, x2 * cos + x1 * sin], axis=-1).astype(x.dtype)


def _partial_rope(x):
    return jnp.concatenate([_rope(x[..., :ROT_DIM], ROPE_THETA), x[..., ROT_DIM:]], axis=-1)


def _map_query_blocks(fn, q):
    b, s = q.shape[:2]
    nb = s // Q_BLOCK
    qb = jnp.swapaxes(q.reshape((b, nb, Q_BLOCK) + q.shape[2:]), 0, 1)
    out = jnp.swapaxes(lax.map(fn, qb), 0, 1)
    return out.reshape((b, s) + out.shape[3:])


def _mla_attention(q, k, v):
    scale = MLA_QK ** -0.5

    def blk(qb):
        sc = jnp.einsum('bqhd,bkhd->bhqk', qb, k).astype(jnp.float32) * scale
        p = jax.nn.softmax(sc, axis=-1).astype(v.dtype)
        return jnp.einsum('bhqk,bkhd->bqhd', p, v)

    return _map_query_blocks(blk, q)


def _diff_attention(q, k, v, lam):
    scale = DIFF_HD ** -0.5

    def blk(qb):
        sc = jnp.einsum('bqhcd,bkhcd->bhcqk', qb, k).astype(jnp.float32) * scale
        p = jax.nn.softmax(sc, axis=-1)
        a = (p[:, :, 0] - lam * p[:, :, 1]).astype(v.dtype)
        return jnp.einsum('bhqk,bkhd->bqhd', a, v)

    return _map_query_blocks(blk, q)


def _peer(x, w_q, keys, u, v):
    b, s, d = x.shape
    t = b * s
    xt = x.reshape(t, d)
    q = (xt @ w_q).reshape(t, PEER_HEADS, 2, PEER_HALF)
    sc = jnp.einsum('thcd,hcnd->thcn', q, keys).astype(jnp.float32)
    s1, i1 = lax.top_k(sc[:, :, 0], PEER_TOPK)
    s2, i2 = lax.top_k(sc[:, :, 1], PEER_TOPK)
    cand = (s1[..., :, None] + s2[..., None, :]).reshape(t, PEER_HEADS, PEER_TOPK * PEER_TOPK)
    top_s, top_pos = lax.top_k(cand, PEER_TOPK)
    e1 = jnp.take_along_axis(i1, top_pos // PEER_TOPK, axis=-1)
    e2 = jnp.take_along_axis(i2, top_pos % PEER_TOPK, axis=-1)
    idx = e1 * N_KEYS + e2
    g = jax.nn.softmax(top_s, axis=-1).astype(x.dtype)
    nb = t // TOK_BLOCK

    def blk(a):
        xb, ib, gb = a
        ug = u[ib]
        hb = jax.nn.gelu(jnp.einsum('td,thkd->thk', xb, ug), approximate=False)
        return jnp.einsum('thk,thkd->td', gb * hb, v[ib])

    out = lax.map(blk, (xt.reshape(nb, TOK_BLOCK, d),
                        idx.reshape(nb, TOK_BLOCK, PEER_HEADS, PEER_TOPK),
                        g.reshape(nb, TOK_BLOCK, PEER_HEADS, PEER_TOPK)))
    return out.reshape(b, s, d)


def _trunk(x, attn_norm, w_in, q_a_norm, w_q_b, kv_a_norm, w_kv_b, mla_q_norm, mla_k_norm,
           diff_q_norm, diff_k_norm, lambda_q1, lambda_k1, lambda_q2, lambda_k2, diff_out_norm,
           w_out, ffn_norm, w_peer_q, peer_keys, peer_u, peer_v):
    b, s, _ = x.shape
    split_points = np.cumsum(IN_SPLITS)[:-1].tolist()
    for l in range(DEPTH):
        xn = _rms_norm(x, attn_norm[l])
        q_a, kv_a, k_rope, dq, dk, dv = jnp.split(xn @ w_in[l], split_points, axis=-1)
        q = (_rms_norm(q_a, q_a_norm[l]) @ w_q_b[l]).reshape(b, s, MLA_HEADS, MLA_QK)
        kv = (_rms_norm(kv_a, kv_a_norm[l]) @ w_kv_b[l]).reshape(b, s, MLA_HEADS, MLA_NOPE + MLA_V)
        k_nope, v_mla = kv[..., :MLA_NOPE], kv[..., MLA_NOPE:]
        k = jnp.concatenate([k_nope, jnp.broadcast_to(k_rope[:, :, None, :], (b, s, MLA_HEADS, MLA_ROPE))], axis=-1)
        q = _rms_norm(q, mla_q_norm[l])
        k = _rms_norm(k, mla_k_norm[l])
        q = jnp.concatenate([q[..., :MLA_NOPE], _rope(q[..., MLA_NOPE:], MLA_ROPE_THETA)], axis=-1)
        k = jnp.concatenate([k[..., :MLA_NOPE], _rope(k[..., MLA_NOPE:], MLA_ROPE_THETA)], axis=-1)
        o_mla = _mla_attention(q, k, v_mla).reshape(b, s, MLA_HEADS * MLA_V)
        dq = _partial_rope(_rms_norm(dq.reshape(b, s, DIFF_HEADS * 2, DIFF_HD), diff_q_norm[l]))
        dk = _partial_rope(_rms_norm(dk.reshape(b, s, DIFF_HEADS * 2, DIFF_HD), diff_k_norm[l]))
        dq = dq.reshape(b, s, DIFF_HEADS, 2, DIFF_HD)
        dk = dk.reshape(b, s, DIFF_HEADS, 2, DIFF_HD)
        lam_init = 0.8 - 0.6 * math.exp(-0.3 * l)
        lam = (jnp.exp(jnp.sum(lambda_q1[l].astype(jnp.float32) * lambda_k1[l].astype(jnp.float32)))
               - jnp.exp(jnp.sum(lambda_q2[l].astype(jnp.float32) * lambda_k2[l].astype(jnp.float32)))
               + lam_init)
        o_diff = _diff_attention(dq, dk, dv.reshape(b, s, DIFF_HEADS, DIFF_V), lam)
        o_diff = (_rms_norm(o_diff, diff_out_norm[l]) * (1.0 - lam_init)).reshape(b, s, DIFF_HEADS * DIFF_V)
        x = x + jnp.concatenate([o_mla, o_diff], axis=-1) @ w_out[l]
        x = x + _peer(_rms_norm(x, ffn_norm[l]), w_peer_q[l], peer_keys[l], peer_u[l], peer_v[l])
    return x


def setup_inputs(seed: int = 0) -> dict:
    key = jax.random.key(seed)
    ks = jax.random.split(key, 24)
    f32 = jnp.float32

    def nrm(k, shape, scale):
        return jax.random.normal(k, shape, f32) * scale

    def gain(k, shape):
        return 1.0 + 0.02 * jax.random.normal(k, shape, f32)

    return {
        "x_prompt": nrm(ks[0], (BATCH, SEQ, D_MODEL), 1.0),
        "x_sample": nrm(ks[1], (DEC_BATCH, DEC_SEQ, D_MODEL), 1.0),
        "attn_norm": gain(ks[2], (DEPTH, D_MODEL)),
        "w_in": nrm(ks[3], (DEPTH, D_MODEL, IN_WIDTH), D_MODEL ** -0.5),
        "q_a_norm": gain(ks[4], (DEPTH, Q_LORA)),
        "w_q_b": nrm(ks[5], (DEPTH, Q_LORA, MLA_HEADS * MLA_QK), Q_LORA ** -0.5),
        "kv_a_norm": gain(ks[6], (DEPTH, KV_LORA)),
        "w_kv_b": nrm(ks[7], (DEPTH, KV_LORA, MLA_HEADS * (MLA_NOPE + MLA_V)), KV_LORA ** -0.5),
        "mla_q_norm": gain(ks[8], (DEPTH, MLA_QK)),
        "mla_k_norm": gain(ks[9], (DEPTH, MLA_QK)),
        "diff_q_norm": gain(ks[10], (DEPTH, DIFF_HD)),
        "diff_k_norm": gain(ks[11], (DEPTH, DIFF_HD)),
        "lambda_q1": nrm(ks[12], (DEPTH, DIFF_HD), 0.1),
        "lambda_k1": nrm(ks[13], (DEPTH, DIFF_HD), 0.1),
        "lambda_q2": nrm(ks[14], (DEPTH, DIFF_HD), 0.1),
        "lambda_k2": nrm(ks[15], (DEPTH, DIFF_HD), 0.1),
        "diff_out_norm": gain(ks[16], (DEPTH, DIFF_V)),
        "w_out": nrm(ks[17], (DEPTH, MIX_WIDTH, D_MODEL), MIX_WIDTH ** -0.5),
        "ffn_norm": gain(ks[18], (DEPTH, D_MODEL)),
        "w_peer_q": nrm(ks[19], (DEPTH, D_MODEL, PEER_HEADS * PEER_DK), D_MODEL ** -0.5),
        "peer_keys": nrm(ks[20], (DEPTH, PEER_HEADS, 2, N_KEYS, PEER_HALF), PEER_HALF ** -0.5),
        "peer_u": nrm(ks[21], (DEPTH, N_EXPERTS, D_MODEL), D_MODEL ** -0.5),
        "peer_v": nrm(ks[22], (DEPTH, N_EXPERTS, D_MODEL), PEER_HEADS ** -0.5),
    }


def reference(x_prompt, x_sample, attn_norm, w_in, q_a_norm, w_q_b, kv_a_norm, w_kv_b,
              mla_q_norm, mla_k_norm, diff_q_norm, diff_k_norm, lambda_q1, lambda_k1,
              lambda_q2, lambda_k2, diff_out_norm, w_out, ffn_norm, w_peer_q, peer_keys,
              peer_u, peer_v):
    y_prompt = _trunk(x_prompt, attn_norm, w_in, q_a_norm, w_q_b, kv_a_norm, w_kv_b,
                      mla_q_norm, mla_k_norm, diff_q_norm, diff_k_norm, lambda_q1, lambda_k1,
                      lambda_q2, lambda_k2, diff_out_norm, w_out, ffn_norm, w_peer_q, peer_keys,
                      peer_u, peer_v)
    y_sample = _trunk(x_sample, attn_norm, w_in, q_a_norm, w_q_b, kv_a_norm, w_kv_b,
                      mla_q_norm, mla_k_norm, diff_q_norm, diff_k_norm, lambda_q1, lambda_k1,
                      lambda_q2, lambda_k2, diff_out_norm, w_out, ffn_norm, w_peer_q, peer_keys,
                      peer_u, peer_v)
    return (y_prompt, y_sample)
```

```python
import functools
import math

import jax
import jax.numpy as jnp
from jax import lax
from jax.experimental import pallas as pl
from jax.experimental.pallas import tpu as pltpu

F32 = jnp.float32
BF16 = jnp.bfloat16

D_MODEL = 4096
MLA_HEADS = 16
MLA_NOPE = 128
MLA_ROPE = 64
MLA_QK = MLA_NOPE + MLA_ROPE
MLA_V = 128
MLA_PAD = 256
Q_LORA = 768
KV_LORA = 512
MLA_ROPE_THETA = 10000.0
DIFF_HEADS = 8
DIFF_HD = 128
DIFF_V = 2 * DIFF_HD
ROT_DIM = DIFF_HD // 4
ROPE_THETA = 500000.0
PEER_HEADS = 8
N_KEYS = 128
N_EXPERTS = N_KEYS * N_KEYS
PEER_TOPK = 16
EPS = 1e-6

DV_W = DIFF_HEADS * DIFF_V
DQ_W = DIFF_HEADS * 2 * DIFF_HD
OFF_DV = 0
OFF_QA = OFF_DV + DV_W
OFF_KVA = OFF_QA + Q_LORA
OFF_KR = OFF_KVA + KV_LORA
OFF_DQ = OFF_KR + MLA_ROPE
OFF_DK = OFF_DQ + DQ_W
IN_WIDTH = OFF_DK + DQ_W

LANES = 128
VMEM_LIMIT_BYTES = 56 * 1024 * 1024
LOG2E = 1.4426950408889634
NEG_BIG = -1e30
POS_BIG = 1e30


def _params(*sem):
    return pltpu.CompilerParams(dimension_semantics=sem, vmem_limit_bytes=VMEM_LIMIT_BYTES)


def _lane_tile(g, width):
    reps = width // LANES
    return g if reps == 1 else jnp.concatenate([g] * reps, axis=1)


def _rms_rows(x, g, n):
    r = lax.rsqrt(jnp.sum(x * x, axis=0, keepdims=True) * (1.0 / n) + EPS)
    return x * r * g


def _norm_t_kernel(x_ref, g_ref, o_ref):
    x = x_ref[...]
    r = lax.rsqrt(jnp.sum(x * x, axis=0, keepdims=True) * (1.0 / x.shape[0]) + EPS)
    g = g_ref[...]
    for c in range(x.shape[1] // LANES):
        sl = slice(c * LANES, (c + 1) * LANES)
        o_ref[:, sl] = (x[:, sl] * r[:, sl] * g).astype(o_ref.dtype)


def _norm_t(x_t, g_rep, tt):
    d, t = x_t.shape
    return pl.pallas_call(
        _norm_t_kernel,
        grid=(t // tt,),
        in_specs=[pl.BlockSpec((d, tt), lambda i: (0, i)),
                  pl.BlockSpec((d, LANES), lambda i: (0, 0))],
        out_specs=pl.BlockSpec((d, tt), lambda i: (0, i)),
        out_shape=jax.ShapeDtypeStruct((d, t), BF16),
        compiler_params=_params("parallel"),
        name="norm_t",
    )(x_t, g_rep)


def _matmul_t_kernel(*refs, n_pairs, has_res):
    o_ref = refs[-1]
    acc = None
    for p in range(n_pairs):
        d = jnp.dot(refs[2 * p][...], refs[2 * p + 1][...], preferred_element_type=F32)
        acc = d if acc is None else acc + d
    if has_res:
        acc = refs[2 * n_pairs][...] + acc
    o_ref[...] = acc.astype(o_ref.dtype)


def _matmul_t(pairs, res, out_dtype, tt, tn, name):
    n = pairs[0][0].shape[0]
    t = pairs[0][1].shape[1]
    in_specs, args = [], []
    for w, a in pairs:
        in_specs.append(pl.BlockSpec((tn, w.shape[1]), lambda i, j: (j, 0)))
        in_specs.append(pl.BlockSpec((a.shape[0], tt), lambda i, j: (0, i)))
        args += [w, a]
    if res is not None:
        in_specs.append(pl.BlockSpec((tn, tt), lambda i, j: (j, i)))
        args.append(res)
    return pl.pallas_call(
        functools.partial(_matmul_t_kernel, n_pairs=len(pairs), has_res=res is not None),
        grid=(t // tt, n // tn),
        in_specs=in_specs,
        out_specs=pl.BlockSpec((tn, tt), lambda i, j: (j, i)),
        out_shape=jax.ShapeDtypeStruct((n, t), out_dtype),
        compiler_params=_params("parallel", "parallel"),
        name=name,
    )(*args)


def _rope_rows(x1, x2, cos, sin):
    return x1 * cos - x2 * sin, x2 * cos + x1 * sin


def _prep_kernel(y_ref, wq_ref, wkv_ref, gqa_ref, gkva_ref, gq_ref, gk_ref, gdq_ref, gdk_ref,
                 cm_ref, sm_ref, cd_ref, sd_ref,
                 q_ref, k_ref, v_ref, dq_ref, dk_ref):
    tt = y_ref.shape[1]
    cm, sm = cm_ref[...], sm_ref[...]
    cd, sd = cd_ref[...], sd_ref[...]
    half = MLA_ROPE // 2
    zpad = jnp.zeros((MLA_PAD - MLA_QK, tt), F32)
    q_scale = (MLA_QK ** -0.5) * LOG2E
    d_scale = (DIFF_HD ** -0.5) * LOG2E

    qa = y_ref[OFF_QA:OFF_QA + Q_LORA, :].astype(F32)
    qan = _rms_rows(qa, _lane_tile(gqa_ref[...], tt), Q_LORA).astype(BF16)
    q = jnp.dot(wq_ref[...], qan, preferred_element_type=F32)
    gq = _lane_tile(gq_ref[...], tt)
    for h in range(MLA_HEADS):
        qn = _rms_rows(q[h * MLA_QK:(h + 1) * MLA_QK], gq, MLA_QK)
        o1, o2 = _rope_rows(qn[MLA_NOPE:MLA_NOPE + half], qn[MLA_NOPE + half:], cm, sm)
        q_ref[h] = (jnp.concatenate([qn[:MLA_NOPE], o1, o2, zpad], axis=0) * q_scale).astype(BF16)

    kva = y_ref[OFF_KVA:OFF_KVA + KV_LORA, :].astype(F32)
    kvan = _rms_rows(kva, _lane_tile(gkva_ref[...], tt), KV_LORA).astype(BF16)
    kv = jnp.dot(wkv_ref[...], kvan, preferred_element_type=F32)
    kr = y_ref[OFF_KR:OFF_KR + MLA_ROPE, :].astype(F32)
    kr_ss = jnp.sum(kr * kr, axis=0, keepdims=True)
    gk = _lane_tile(gk_ref[...], tt)
    krg = kr * gk[MLA_NOPE:]
    kr1, kr2 = _rope_rows(krg[:half], krg[half:], cm, sm)
    for h in range(MLA_HEADS):
        base = h * (MLA_NOPE + MLA_V)
        kn = kv[base:base + MLA_NOPE]
        r = lax.rsqrt((jnp.sum(kn * kn, axis=0, keepdims=True) + kr_ss) * (1.0 / MLA_QK) + EPS)
        kfull = jnp.concatenate([kn * r * gk[:MLA_NOPE], kr1 * r, kr2 * r, zpad], axis=0)
        k_ref[h] = jnp.transpose(kfull).astype(BF16)
        v_ref[h] = kv[base + MLA_NOPE:base + MLA_NOPE + MLA_V].astype(BF16)

    gdq = _lane_tile(gdq_ref[...], tt)
    gdk = _lane_tile(gdk_ref[...], tt)
    hr = ROT_DIM // 2
    for j in range(2 * DIFF_HEADS):
        xq = _rms_rows(y_ref[OFF_DQ + j * DIFF_HD:OFF_DQ + (j + 1) * DIFF_HD, :].astype(F32), gdq, DIFF_HD)
        o1, o2 = _rope_rows(xq[:hr], xq[hr:ROT_DIM], cd, sd)
        dq_ref[j] = (jnp.concatenate([o1, o2, xq[ROT_DIM:]], axis=0) * d_scale).astype(BF16)
        xk = _rms_rows(y_ref[OFF_DK + j * DIFF_HD:OFF_DK + (j + 1) * DIFF_HD, :].astype(F32), gdk, DIFF_HD)
        o1, o2 = _rope_rows(xk[:hr], xk[hr:ROT_DIM], cd, sd)
        dk_ref[j] = jnp.transpose(jnp.concatenate([o1, o2, xk[ROT_DIM:]], axis=0)).astype(BF16)


def _prep(y_t, wq_t, wkv_t, gains, tables, seq, tt):
    t = y_t.shape[1]
    nseq = seq // tt
    full = lambda a: pl.BlockSpec(a.shape, lambda i: (0,) * a.ndim)
    tab = lambda a: pl.BlockSpec((a.shape[0], tt), lambda i: (0, i % nseq))
    nh, nd = MLA_HEADS, 2 * DIFF_HEADS
    return pl.pallas_call(
        _prep_kernel,
        grid=(t // tt,),
        in_specs=[pl.BlockSpec((IN_WIDTH, tt), lambda i: (0, i)), full(wq_t), full(wkv_t)]
                 + [full(g) for g in gains] + [tab(a) for a in tables],
        out_specs=[pl.BlockSpec((nh, MLA_PAD, tt), lambda i: (0, 0, i)),
                   pl.BlockSpec((nh, tt, MLA_PAD), lambda i: (0, i, 0)),
                   pl.BlockSpec((nh, MLA_V, tt), lambda i: (0, 0, i)),
                   pl.BlockSpec((nd, DIFF_HD, tt), lambda i: (0, 0, i)),
                   pl.BlockSpec((nd, tt, DIFF_HD), lambda i: (0, i, 0))],
        out_shape=[jax.ShapeDtypeStruct((nh, MLA_PAD, t), BF16),
                   jax.ShapeDtypeStruct((nh, t, MLA_PAD), BF16),
                   jax.ShapeDtypeStruct((nh, MLA_V, t), BF16),
                   jax.ShapeDtypeStruct((nd, DIFF_HD, t), BF16),
                   jax.ShapeDtypeStruct((nd, t, DIFF_HD), BF16)],
        compiler_params=_params("parallel"),
        name="qkv_prep",
    )(y_t, wq_t, wkv_t, *gains, *tables)


def _softmax_step(s, m_ref, l_ref):
    m_prev = m_ref[...]
    m_new = jnp.maximum(m_prev, jnp.max(s, axis=0, keepdims=True))
    p = jnp.exp2(s - m_new)
    alpha = jnp.exp2(m_prev - m_new)
    l_ref[...] = alpha * l_ref[...] + jnp.sum(p, axis=0, keepdims=True)
    m_ref[...] = m_new
    return p, alpha


def _mla_attn_kernel(q_ref, k_ref, v_ref, o_ref, m_ref, l_ref, acc_ref, *, tk):
    q = q_ref[0]
    m_ref[...] = jnp.full(m_ref.shape, NEG_BIG, F32)
    l_ref[...] = jnp.zeros(l_ref.shape, F32)
    acc_ref[...] = jnp.zeros(acc_ref.shape, F32)

    def body(c, carry):
        off = pl.multiple_of(c * tk, tk)
        s = jnp.dot(k_ref[0, pl.ds(off, tk), :], q, preferred_element_type=F32)
        p, alpha = _softmax_step(s, m_ref, l_ref)
        pv = jnp.dot(v_ref[0, :, pl.ds(off, tk)], p.astype(BF16), preferred_element_type=F32)
        acc_ref[...] = alpha * acc_ref[...] + pv
        return carry

    lax.fori_loop(0, k_ref.shape[1] // tk, body, 0)
    o_ref[...] = (acc_ref[...] / l_ref[...]).astype(o_ref.dtype)


def _mla_attn(q_t, k, v_t, batch, seq, tq, tk):
    nh, _, t = q_t.shape
    nq = seq // tq
    return pl.pallas_call(
        functools.partial(_mla_attn_kernel, tk=tk),
        grid=(batch, nh, nq),
        in_specs=[pl.BlockSpec((1, MLA_PAD, tq), lambda b, h, i: (h, 0, b * nq + i)),
                  pl.BlockSpec((1, seq, MLA_PAD), lambda b, h, i: (h, b, 0)),
                  pl.BlockSpec((1, MLA_V, seq), lambda b, h, i: (h, 0, b))],
        out_specs=pl.BlockSpec((MLA_V, tq), lambda b, h, i: (h, b * nq + i)),
        out_shape=jax.ShapeDtypeStruct((nh * MLA_V, t), BF16),
        scratch_shapes=[pltpu.VMEM((1, tq), F32), pltpu.VMEM((1, tq), F32), pltpu.VMEM((MLA_V, tq), F32)],
        compiler_params=_params("parallel", "parallel", "parallel"),
        name="mla_attn",
    )(q_t, k, v_t)


def _diff_attn_kernel(q1_ref, q2_ref, k1_ref, k2_ref, v_ref, lam_ref, g_ref, o_ref,
                      m1_ref, l1_ref, acc1_ref, m2_ref, l2_ref, acc2_ref, *, tk, lam_init):
    q1, q2 = q1_ref[0], q2_ref[0]
    for m_ref, l_ref, acc_ref in ((m1_ref, l1_ref, acc1_ref), (m2_ref, l2_ref, acc2_ref)):
        m_ref[...] = jnp.full(m_ref.shape, NEG_BIG, F32)
        l_ref[...] = jnp.zeros(l_ref.shape, F32)
        acc_ref[...] = jnp.zeros(acc_ref.shape, F32)

    def body(c, carry):
        off = pl.multiple_of(c * tk, tk)
        vc = v_ref[:, pl.ds(off, tk)]
        for q, k_ref, m_ref, l_ref, acc_ref in ((q1, k1_ref, m1_ref, l1_ref, acc1_ref),
                                                (q2, k2_ref, m2_ref, l2_ref, acc2_ref)):
            s = jnp.dot(k_ref[0, pl.ds(off, tk), :], q, preferred_element_type=F32)
            p, alpha = _softmax_step(s, m_ref, l_ref)
            acc_ref[...] = alpha * acc_ref[...] + jnp.dot(vc, p.astype(BF16), preferred_element_type=F32)
        return carry

    lax.fori_loop(0, k1_ref.shape[1] // tk, body, 0)
    lv = lam_ref[...]
    lam = (jnp.exp(jnp.sum(lv[0:1] * lv[1:2], axis=1, keepdims=True))
           - jnp.exp(jnp.sum(lv[2:3] * lv[3:4], axis=1, keepdims=True)) + lam_init)
    o = acc1_ref[...] / l1_ref[...] - lam * (acc2_ref[...] / l2_ref[...])
    g = _lane_tile(g_ref[...], o.shape[1])
    o_ref[...] = (_rms_rows(o, g, DIFF_V) * (1.0 - lam_init)).astype(o_ref.dtype)


def _diff_attn(dq_t, dk, y_t, lam_vec, g_out, batch, seq, tq, tk, lam_init):
    t = dq_t.shape[2]
    nq = seq // tq
    return pl.pallas_call(
        functools.partial(_diff_attn_kernel, tk=tk, lam_init=lam_init),
        grid=(batch, DIFF_HEADS, nq),
        in_specs=[pl.BlockSpec((1, DIFF_HD, tq), lambda b, h, i: (2 * h, 0, b * nq + i)),
                  pl.BlockSpec((1, DIFF_HD, tq), lambda b, h, i: (2 * h + 1, 0, b * nq + i)),
                  pl.BlockSpec((1, seq, DIFF_HD), lambda b, h, i: (2 * h, b, 0)),
                  pl.BlockSpec((1, seq, DIFF_HD), lambda b, h, i: (2 * h + 1, b, 0)),
                  pl.BlockSpec((DIFF_V, seq), lambda b, h, i: (h, b)),
                  pl.BlockSpec(lam_vec.shape, lambda b, h, i: (0, 0)),
                  pl.BlockSpec(g_out.shape, lambda b, h, i: (0, 0))],
        out_specs=pl.BlockSpec((DIFF_V, tq), lambda b, h, i: (h, b * nq + i)),
        out_shape=jax.ShapeDtypeStruct((DIFF_HEADS * DIFF_V, t), BF16),
        scratch_shapes=[pltpu.VMEM((1, tq), F32), pltpu.VMEM((1, tq), F32), pltpu.VMEM((DIFF_V, tq), F32),
                        pltpu.VMEM((1, tq), F32), pltpu.VMEM((1, tq), F32), pltpu.VMEM((DIFF_V, tq), F32)],
        compiler_params=_params("parallel", "parallel", "parallel"),
        name="diff_attn",
    )(dq_t, dq_t, dk, dk, y_t, lam_vec, g_out)


def _top16_rows(a):
    vals = []
    for _ in range(PEER_TOPK):
        mx = jnp.max(a, axis=0, keepdims=True)
        vals.append(mx)
        a = jnp.where(a >= mx, NEG_BIG, a)
    return vals


def _peer_route_kernel(qp_ref, keys_ref, a1_ref, a2_ref, thr_ref):
    tt = qp_ref.shape[1]
    rows = lax.broadcasted_iota(jnp.int32, (PEER_TOPK, tt), 0)

    def head(h, carry):
        off = pl.multiple_of(h * 2 * N_KEYS, 2 * N_KEYS)
        sc1 = jnp.dot(keys_ref[h, 0], qp_ref[pl.ds(off, N_KEYS), :], preferred_element_type=F32)
        sc2 = jnp.dot(keys_ref[h, 1], qp_ref[pl.ds(off + N_KEYS, N_KEYS), :], preferred_element_type=F32)
        a1 = (sc1 - jnp.max(sc1, axis=0, keepdims=True)) * LOG2E
        a2 = (sc2 - jnp.max(sc2, axis=0, keepdims=True)) * LOG2E
        v1 = _top16_rows(a1)
        v2 = _top16_rows(a2)
        v2m = jnp.zeros((PEER_TOPK, tt), F32)
        for j in range(PEER_TOPK):
            v2m = jnp.where(rows == j, v2[j], v2m)
        cand = jnp.concatenate([v1[i] + v2m for i in range(PEER_TOPK)], axis=0)
        tops = _top16_rows(cand)
        z = tops[0] * 0.0
        for tv in tops:
            z = z + jnp.exp2(tv)
        lz = jnp.log2(z)
        cand_z = jnp.concatenate([(v1[i] - lz) + v2m for i in range(PEER_TOPK)], axis=0)
        thr = jnp.min(jnp.where(cand >= tops[-1], cand_z, POS_BIG), axis=0, keepdims=True)
        a1_ref[h] = a1 - lz
        a2_ref[h] = a2
        thr_ref[h] = thr
        return carry

    lax.fori_loop(0, PEER_HEADS, head, 0)


def _peer_route(qp_t, keys, tt):
    t = qp_t.shape[1]
    return pl.pallas_call(
        _peer_route_kernel,
        grid=(t // tt,),
        in_specs=[pl.BlockSpec((qp_t.shape[0], tt), lambda i: (0, i)),
                  pl.BlockSpec(keys.shape, lambda i: (0, 0, 0, 0))],
        out_specs=[pl.BlockSpec((PEER_HEADS, N_KEYS, tt), lambda i: (0, 0, i)),
                   pl.BlockSpec((PEER_HEADS, N_KEYS, tt), lambda i: (0, 0, i)),
                   pl.BlockSpec((PEER_HEADS, 1, tt), lambda i: (0, 0, i))],
        out_shape=[jax.ShapeDtypeStruct((PEER_HEADS, N_KEYS, t), F32),
                   jax.ShapeDtypeStruct((PEER_HEADS, N_KEYS, t), F32),
                   jax.ShapeDtypeStruct((PEER_HEADS, 1, t), F32)],
        compiler_params=_params("parallel"),
        name="peer_route",
    )(qp_t, keys)


def _peer_dense_kernel(x_ref, u_ref, vt_ref, a1_ref, a2_ref, thr_ref, res_ref, o_ref, w_ref, *, te):
    j = pl.program_id(1)
    tt = x_ref.shape[1]

    @pl.when(j == 0)
    def _():
        o_ref[...] = res_ref[...]

    hid = jnp.dot(u_ref[...], x_ref[...], preferred_element_type=F32)
    for e in range(te // N_KEYS):
        e1 = j * (te // N_KEYS) + e
        gate = jnp.zeros((N_KEYS, tt), F32)
        for h in range(PEER_HEADS):
            s = a2_ref[h] + a1_ref[h, pl.ds(e1, 1), :]
            gate = gate + jnp.where(s >= thr_ref[h], jnp.exp2(s), 0.0)
        hh = hid[e * N_KEYS:(e + 1) * N_KEYS]
        act = 0.5 * hh * (1.0 + lax.erf(hh * (2.0 ** -0.5)))
        w_ref[e * N_KEYS:(e + 1) * N_KEYS, :] = (gate * act).astype(BF16)
    o_ref[...] += jnp.dot(vt_ref[...], w_ref[...], preferred_element_type=F32)


def _peer_dense(xn_t, u, v_t, a1, a2, thr, res_t, tt, te):
    d, t = xn_t.shape
    once = dict(pipeline_mode=pl.Buffered(1))
    return pl.pallas_call(
        functools.partial(_peer_dense_kernel, te=te),
        grid=(t // tt, N_EXPERTS // te),
        in_specs=[pl.BlockSpec((d, tt), lambda i, j: (0, i), **once),
                  pl.BlockSpec((te, d), lambda i, j: (j, 0)),
                  pl.BlockSpec((d, te), lambda i, j: (0, j)),
                  pl.BlockSpec((PEER_HEADS, N_KEYS, tt), lambda i, j: (0, 0, i), **once),
                  pl.BlockSpec((PEER_HEADS, N_KEYS, tt), lambda i, j: (0, 0, i), **once),
                  pl.BlockSpec((PEER_HEADS, 1, tt), lambda i, j: (0, 0, i), **once),
                  pl.BlockSpec((d, tt), lambda i, j: (0, i), **once)],
        out_specs=pl.BlockSpec((d, tt), lambda i, j: (0, i)),
        out_shape=jax.ShapeDtypeStruct((d, t), F32),
        scratch_shapes=[pltpu.VMEM((te, tt), BF16)],
        compiler_params=_params("parallel", "arbitrary"),
        name="peer_dense",
    )(xn_t, u, v_t, a1, a2, thr, res_t)


def _rep(g):
    return jnp.broadcast_to(g.astype(F32)[:, None], (g.shape[0], LANES))


def _rope_tables(seq, dim, theta):
    inv = theta ** (-jnp.arange(0, dim, 2, dtype=F32) / dim)
    ang = inv[:, None] * jnp.arange(seq, dtype=F32)[None, :]
    return jnp.cos(ang), jnp.sin(ang)


def _tile(n, pref):
    return pref if n % pref == 0 else n


def _trunk_t(x_t, batch, seq, p):
    t = x_t.shape[1]
    tt = _tile(t, 512)
    tp = _tile(seq, 256)
    tq = _tile(seq, 512)
    tk = _tile(seq, 512)
    cm, sm = _rope_tables(seq, MLA_ROPE, MLA_ROPE_THETA)
    cd, sd = _rope_tables(seq, ROT_DIM, ROPE_THETA)
    depth = p["w_in"].shape[0]
    for l in range(depth):
        w_in = p["w_in"][l]
        w_in_t = jnp.concatenate([w_in[:, IN_WIDTH - DV_W:], w_in[:, :IN_WIDTH - DV_W]], axis=1).T.astype(BF16)
        wq_t = p["w_q_b"][l].T.astype(BF16)
        wkv_t = p["w_kv_b"][l].T.astype(BF16)
        wo_t = p["w_out"][l].T.astype(BF16)
        wpq_t = p["w_peer_q"][l].T.astype(BF16)
        keys = p["peer_keys"][l].astype(BF16)
        u = p["peer_u"][l].astype(BF16)
        v_t = p["peer_v"][l].T.astype(BF16)
        gains = [_rep(p[k][l]) for k in ("q_a_norm", "kv_a_norm", "mla_q_norm", "mla_k_norm",
                                         "diff_q_norm", "diff_k_norm")]
        lam_vec = jnp.stack([p[k][l].astype(F32) for k in ("lambda_q1", "lambda_k1", "lambda_q2", "lambda_k2")])
        lam_init = 0.8 - 0.6 * math.exp(-0.3 * l)

        xn = _norm_t(x_t, _rep(p["attn_norm"][l]), tt)
        y_t = _matmul_t([(w_in_t, xn)], None, BF16, tt, 576, "in_proj")
        q_t, k, vm_t, dq_t, dk = _prep(y_t, wq_t, wkv_t, gains, (cm, sm, cd, sd), seq, tp)
        o_mla = _mla_attn(q_t, k, vm_t, batch, seq, tq, tk)
        o_diff = _diff_attn(dq_t, dk, y_t, lam_vec, _rep(p["diff_out_norm"][l]), batch, seq, tq, tk, lam_init)
        nm = MLA_HEADS * MLA_V
        x_t = _matmul_t([(wo_t[:, :nm], o_mla), (wo_t[:, nm:], o_diff)], x_t, F32, tt, 512, "out_proj")

        xn2 = _norm_t(x_t, _rep(p["ffn_norm"][l]), tt)
        qp_t = _matmul_t([(wpq_t, xn2)], None, BF16, tt, 512, "peer_query")
        a1, a2, thr = _peer_route(qp_t, keys, _tile(t, 256))
        x_t = _peer_dense(xn2, u, v_t, a1, a2, thr, x_t, tt, 512)
    return x_t


def kernel(x_prompt, x_sample, attn_norm, w_in, q_a_norm, w_q_b, kv_a_norm, w_kv_b, mla_q_norm, mla_k_norm,
           diff_q_norm, diff_k_norm, lambda_q1, lambda_k1, lambda_q2, lambda_k2, diff_out_norm, w_out,
           ffn_norm, w_peer_q, peer_keys, peer_u, peer_v):
    p = dict(attn_norm=attn_norm, w_in=w_in, q_a_norm=q_a_norm, w_q_b=w_q_b, kv_a_norm=kv_a_norm,
             w_kv_b=w_kv_b, mla_q_norm=mla_q_norm, mla_k_norm=mla_k_norm, diff_q_norm=diff_q_norm,
             diff_k_norm=diff_k_norm, lambda_q1=lambda_q1, lambda_k1=lambda_k1, lambda_q2=lambda_q2,
             lambda_k2=lambda_k2, diff_out_norm=diff_out_norm, w_out=w_out, ffn_norm=ffn_norm,
             w_peer_q=w_peer_q, peer_keys=peer_keys, peer_u=peer_u, peer_v=peer_v)
    bp, seq, d = x_prompt.shape
    bs = x_sample.shape[0]
    assert x_sample.shape[1] == seq
    x = jnp.concatenate([x_prompt.reshape(bp * seq, d), x_sample.reshape(bs * seq, d)], axis=0)
    y = _trunk_t(x.T, bp + bs, seq, p).T
    return y[:bp * seq].reshape(bp, seq, d), y[bp * seq:].reshape(bs, seq, d)
```

```python
import functools
import math

import jax
import jax.numpy as jnp
from jax import lax
from jax.experimental import pallas as pl
from jax.experimental.pallas import tpu as pltpu

F32 = jnp.float32
BF16 = jnp.bfloat16

D_MODEL = 4096
MLA_HEADS = 16
MLA_NOPE = 128
MLA_ROPE = 64
MLA_QK = MLA_NOPE + MLA_ROPE
MLA_V = 128
MLA_PAD = 256
Q_LORA = 768
KV_LORA = 512
MLA_ROPE_THETA = 10000.0
DIFF_HEADS = 8
DIFF_HD = 128
DIFF_V = 2 * DIFF_HD
ROT_DIM = DIFF_HD // 4
ROPE_THETA = 500000.0
PEER_HEADS = 8
N_KEYS = 128
N_EXPERTS = N_KEYS * N_KEYS
PEER_TOPK = 16
EPS = 1e-6

DV_W = DIFF_HEADS * DIFF_V
DQ_W = DIFF_HEADS * 2 * DIFF_HD
OFF_DV = 0
OFF_QA = OFF_DV + DV_W
OFF_KVA = OFF_QA + Q_LORA
OFF_KR = OFF_KVA + KV_LORA
OFF_DQ = OFF_KR + MLA_ROPE
OFF_DK = OFF_DQ + DQ_W
IN_WIDTH = OFF_DK + DQ_W

LANES = 128
VMEM_LIMIT_BYTES = 56 * 1024 * 1024
LOG2E = 1.4426950408889634
NEG_BIG = -1e30
POS_BIG = 1e30


def _params(*sem, flags=None):
    return pltpu.CompilerParams(dimension_semantics=sem, vmem_limit_bytes=VMEM_LIMIT_BYTES, flags=flags)


def _lane_tile(g, width):
    reps = width // LANES
    return g if reps == 1 else jnp.concatenate([g] * reps, axis=1)


def _rms_rows(x, g, n):
    r = lax.rsqrt(jnp.sum(x * x, axis=0, keepdims=True) * (1.0 / n) + EPS)
    return x * r * g


def _norm_t_kernel(x_ref, g_ref, o_ref):
    x = x_ref[...]
    r = lax.rsqrt(jnp.sum(x * x, axis=0, keepdims=True) * (1.0 / x.shape[0]) + EPS)
    g = g_ref[...]
    for c in range(x.shape[1] // LANES):
        sl = slice(c * LANES, (c + 1) * LANES)
        o_ref[:, sl] = (x[:, sl] * r[:, sl] * g).astype(o_ref.dtype)


def _norm_t(x_t, g_rep, tt):
    d, t = x_t.shape
    return pl.pallas_call(
        _norm_t_kernel,
        grid=(t // tt,),
        in_specs=[pl.BlockSpec((d, tt), lambda i: (0, i)),
                  pl.BlockSpec((d, LANES), lambda i: (0, 0))],
        out_specs=pl.BlockSpec((d, tt), lambda i: (0, i)),
        out_shape=jax.ShapeDtypeStruct((d, t), BF16),
        compiler_params=_params("parallel"),
        name="norm_t",
    )(x_t, g_rep)


def _matmul_t_kernel(*refs, n_pairs, has_res):
    o_ref = refs[-1]
    acc = None
    for p in range(n_pairs):
        d = jnp.dot(refs[2 * p][...], refs[2 * p + 1][...], preferred_element_type=F32)
        acc = d if acc is None else acc + d
    if has_res:
        acc = refs[2 * n_pairs][...] + acc
    o_ref[...] = acc.astype(o_ref.dtype)


def _matmul_t(pairs, res, out_dtype, tt, tn, name):
    n = pairs[0][0].shape[0]
    t = pairs[0][1].shape[1]
    in_specs, args = [], []
    for w, a in pairs:
        in_specs.append(pl.BlockSpec((tn, w.shape[1]), lambda i, j: (j, 0)))
        in_specs.append(pl.BlockSpec((a.shape[0], tt), lambda i, j: (0, i)))
        args += [w, a]
    if res is not None:
        in_specs.append(pl.BlockSpec((tn, tt), lambda i, j: (j, i)))
        args.append(res)
    return pl.pallas_call(
        functools.partial(_matmul_t_kernel, n_pairs=len(pairs), has_res=res is not None),
        grid=(t // tt, n // tn),
        in_specs=in_specs,
        out_specs=pl.BlockSpec((tn, tt), lambda i, j: (j, i)),
        out_shape=jax.ShapeDtypeStruct((n, t), out_dtype),
        compiler_params=_params("parallel", "parallel"),
        name=name,
    )(*args)


def _rope_rows(x1, x2, cos, sin):
    return x1 * cos - x2 * sin, x2 * cos + x1 * sin


def _prep_kernel(y_ref, wq_ref, wkv_ref, gqa_ref, gkva_ref, gq_ref, gk_ref, gdq_ref, gdk_ref,
                 cm_ref, sm_ref, cd_ref, sd_ref,
                 q_ref, k_ref, v_ref, dq_ref, dk_ref):
    tt = y_ref.shape[1]
    cm, sm = cm_ref[...], sm_ref[...]
    cd, sd = cd_ref[...], sd_ref[...]
    half = MLA_ROPE // 2
    zpad = jnp.zeros((MLA_PAD - MLA_QK, tt), F32)
    q_scale = (MLA_QK ** -0.5) * LOG2E
    d_scale = (DIFF_HD ** -0.5) * LOG2E

    qa = y_ref[OFF_QA:OFF_QA + Q_LORA, :].astype(F32)
    qan = _rms_rows(qa, _lane_tile(gqa_ref[...], tt), Q_LORA).astype(BF16)
    q = jnp.dot(wq_ref[...], qan, preferred_element_type=F32)
    gq = _lane_tile(gq_ref[...], tt)
    for h in range(MLA_HEADS):
        qn = _rms_rows(q[h * MLA_QK:(h + 1) * MLA_QK], gq, MLA_QK)
        o1, o2 = _rope_rows(qn[MLA_NOPE:MLA_NOPE + half], qn[MLA_NOPE + half:], cm, sm)
        q_ref[h] = (jnp.concatenate([qn[:MLA_NOPE], o1, o2, zpad], axis=0) * q_scale).astype(BF16)

    kva = y_ref[OFF_KVA:OFF_KVA + KV_LORA, :].astype(F32)
    kvan = _rms_rows(kva, _lane_tile(gkva_ref[...], tt), KV_LORA).astype(BF16)
    kv = jnp.dot(wkv_ref[...], kvan, preferred_element_type=F32)
    kr = y_ref[OFF_KR:OFF_KR + MLA_ROPE, :].astype(F32)
    kr_ss = jnp.sum(kr * kr, axis=0, keepdims=True)
    gk = _lane_tile(gk_ref[...], tt)
    krg = kr * gk[MLA_NOPE:]
    kr1, kr2 = _rope_rows(krg[:half], krg[half:], cm, sm)
    for h in range(MLA_HEADS):
        base = h * (MLA_NOPE + MLA_V)
        kn = kv[base:base + MLA_NOPE]
        r = lax.rsqrt((jnp.sum(kn * kn, axis=0, keepdims=True) + kr_ss) * (1.0 / MLA_QK) + EPS)
        kfull = jnp.concatenate([kn * r * gk[:MLA_NOPE], kr1 * r, kr2 * r, zpad], axis=0)
        k_ref[h] = jnp.transpose(kfull).astype(BF16)
        v_ref[h] = kv[base + MLA_NOPE:base + MLA_NOPE + MLA_V].astype(BF16)

    gdq = _lane_tile(gdq_ref[...], tt)
    gdk = _lane_tile(gdk_ref[...], tt)
    hr = ROT_DIM // 2
    for j in range(2 * DIFF_HEADS):
        xq = _rms_rows(y_ref[OFF_DQ + j * DIFF_HD:OFF_DQ + (j + 1) * DIFF_HD, :].astype(F32), gdq, DIFF_HD)
        o1, o2 = _rope_rows(xq[:hr], xq[hr:ROT_DIM], cd, sd)
        dq_ref[j] = (jnp.concatenate([o1, o2, xq[ROT_DIM:]], axis=0) * d_scale).astype(BF16)
        xk = _rms_rows(y_ref[OFF_DK + j * DIFF_HD:OFF_DK + (j + 1) * DIFF_HD, :].astype(F32), gdk, DIFF_HD)
        o1, o2 = _rope_rows(xk[:hr], xk[hr:ROT_DIM], cd, sd)
        dk_ref[j] = jnp.transpose(jnp.concatenate([o1, o2, xk[ROT_DIM:]], axis=0)).astype(BF16)


def _prep(y_t, wq_t, wkv_t, gains, tables, seq, tt):
    t = y_t.shape[1]
    nseq = seq // tt
    full = lambda a: pl.BlockSpec(a.shape, lambda i: (0,) * a.ndim)
    tab = lambda a: pl.BlockSpec((a.shape[0], tt), lambda i: (0, i % nseq))
    nh, nd = MLA_HEADS, 2 * DIFF_HEADS
    return pl.pallas_call(
        _prep_kernel,
        grid=(t // tt,),
        in_specs=[pl.BlockSpec((IN_WIDTH, tt), lambda i: (0, i)), full(wq_t), full(wkv_t)]
                 + [full(g) for g in gains] + [tab(a) for a in tables],
        out_specs=[pl.BlockSpec((nh, MLA_PAD, tt), lambda i: (0, 0, i)),
                   pl.BlockSpec((nh, tt, MLA_PAD), lambda i: (0, i, 0)),
                   pl.BlockSpec((nh, MLA_V, tt), lambda i: (0, 0, i)),
                   pl.BlockSpec((nd, DIFF_HD, tt), lambda i: (0, 0, i)),
                   pl.BlockSpec((nd, tt, DIFF_HD), lambda i: (0, i, 0))],
        out_shape=[jax.ShapeDtypeStruct((nh, MLA_PAD, t), BF16),
                   jax.ShapeDtypeStruct((nh, t, MLA_PAD), BF16),
                   jax.ShapeDtypeStruct((nh, MLA_V, t), BF16),
                   jax.ShapeDtypeStruct((nd, DIFF_HD, t), BF16),
                   jax.ShapeDtypeStruct((nd, t, DIFF_HD), BF16)],
        compiler_params=_params("parallel"),
        name="qkv_prep",
    )(y_t, wq_t, wkv_t, *gains, *tables)


class _Stream:
    def __init__(self, q, k_ref, s_ref, p_ref, a_ref, m_ref, l_ref, acc_ref):
        self.q, self.k_ref = q, k_ref
        self.s_ref, self.p_ref, self.a_ref = s_ref, p_ref, a_ref
        self.m_ref, self.l_ref, self.acc_ref = m_ref, l_ref, acc_ref

    def init(self):
        self.m_ref[...] = jnp.full(self.m_ref.shape, NEG_BIG, F32)
        self.l_ref[...] = jnp.zeros(self.l_ref.shape, F32)
        self.acc_ref[...] = jnp.zeros(self.acc_ref.shape, F32)

    def qk(self, off, tk, slot):
        self.s_ref[slot] = jnp.dot(self.k_ref[0, pl.ds(off, tk), :], self.q, preferred_element_type=F32)

    def softmax(self, slot):
        s = self.s_ref[slot]
        m_prev = self.m_ref[...]
        m_new = jnp.maximum(m_prev, jnp.max(s, axis=0, keepdims=True))
        p = jnp.exp2(s - m_new)
        alpha = jnp.exp2(m_prev - m_new)
        self.l_ref[...] = alpha * self.l_ref[...] + jnp.sum(p, axis=0, keepdims=True)
        self.m_ref[...] = m_new
        self.a_ref[slot] = alpha
        self.p_ref[slot] = p.astype(BF16)

    def pv(self, vc, slot):
        pv = jnp.dot(vc, self.p_ref[slot], preferred_element_type=F32)
        self.acc_ref[...] = self.a_ref[slot] * self.acc_ref[...] + pv

    def result(self):
        return self.acc_ref[...] / self.l_ref[...]


def _attend(streams, v_chunk, n, tk):
    def qk(i, slot):
        off = pl.multiple_of(i * tk, tk)
        for st in streams:
            st.qk(off, tk, slot)

    def pv(i, slot):
        vc = v_chunk(pl.multiple_of(i * tk, tk))
        for st in streams:
            st.pv(vc, slot)

    def softmax(slot):
        for st in streams:
            st.softmax(slot)

    for st in streams:
        st.init()
    qk(0, 0)
    qk(1, 1)
    softmax(0)

    def pair(j, carry):
        i = 2 * j + 1
        qk(i + 1, 0)
        pv(i - 1, 0)
        softmax(1)
        qk(i + 2, 1)
        pv(i, 1)
        softmax(0)
        return carry

    lax.fori_loop(0, (n - 2) // 2, pair, 0)
    pv(n - 2, 0)
    softmax(1)
    pv(n - 1, 1)


def _stream_scratch(rows_v, tq, tk):
    return [pltpu.VMEM((2, tk, tq), F32), pltpu.VMEM((2, tk, tq), BF16), pltpu.VMEM((2, 1, tq), F32),
            pltpu.VMEM((1, tq), F32), pltpu.VMEM((1, tq), F32), pltpu.VMEM((rows_v, tq), F32)]


def _mla_attn_kernel(q_ref, k_ref, v_ref, o_ref, *scratch, tk):
    st = _Stream(q_ref[0], k_ref, *scratch)
    _attend([st], lambda off: v_ref[0, :, pl.ds(off, tk)], k_ref.shape[1] // tk, tk)
    o_ref[...] = st.result().astype(o_ref.dtype)


def _mla_attn(q_t, k, v_t, batch, seq, tq, tk):
    nh, _, t = q_t.shape
    nq = seq // tq
    return pl.pallas_call(
        functools.partial(_mla_attn_kernel, tk=tk),
        grid=(batch, nh, nq),
        in_specs=[pl.BlockSpec((1, MLA_PAD, tq), lambda b, h, i: (h, 0, b * nq + i)),
                  pl.BlockSpec((1, seq, MLA_PAD), lambda b, h, i: (h, b, 0)),
                  pl.BlockSpec((1, MLA_V, seq), lambda b, h, i: (h, 0, b))],
        out_specs=pl.BlockSpec((MLA_V, tq), lambda b, h, i: (h, b * nq + i)),
        out_shape=jax.ShapeDtypeStruct((nh * MLA_V, t), BF16),
        scratch_shapes=_stream_scratch(MLA_V, tq, tk),
        compiler_params=_params("parallel", "parallel", "parallel"),
        name="mla_attn",
    )(q_t, k, v_t)


def _diff_attn_kernel(q1_ref, q2_ref, k1_ref, k2_ref, v_ref, lam_ref, g_ref, o_ref, *scratch, tk, lam_init):
    st1 = _Stream(q1_ref[0], k1_ref, *scratch[:6])
    st2 = _Stream(q2_ref[0], k2_ref, *scratch[6:])
    _attend([st1, st2], lambda off: v_ref[:, pl.ds(off, tk)], k1_ref.shape[1] // tk, tk)
    lv = lam_ref[...]
    lam = (jnp.exp(jnp.sum(lv[0:1] * lv[1:2], axis=1, keepdims=True))
           - jnp.exp(jnp.sum(lv[2:3] * lv[3:4], axis=1, keepdims=True)) + lam_init)
    o = st1.result() - lam * st2.result()
    g = _lane_tile(g_ref[...], o.shape[1])
    o_ref[...] = (_rms_rows(o, g, DIFF_V) * (1.0 - lam_init)).astype(o_ref.dtype)


def _diff_attn(dq_t, dk, y_t, lam_vec, g_out, batch, seq, tq, tk, lam_init):
    t = dq_t.shape[2]
    nq = seq // tq
    return pl.pallas_call(
        functools.partial(_diff_attn_kernel, tk=tk, lam_init=lam_init),
        grid=(batch, DIFF_HEADS, nq),
        in_specs=[pl.BlockSpec((1, DIFF_HD, tq), lambda b, h, i: (2 * h, 0, b * nq + i)),
                  pl.BlockSpec((1, DIFF_HD, tq), lambda b, h, i: (2 * h + 1, 0, b * nq + i)),
                  pl.BlockSpec((1, seq, DIFF_HD), lambda b, h, i: (2 * h, b, 0)),
                  pl.BlockSpec((1, seq, DIFF_HD), lambda b, h, i: (2 * h + 1, b, 0)),
                  pl.BlockSpec((DIFF_V, seq), lambda b, h, i: (h, b)),
                  pl.BlockSpec(lam_vec.shape, lambda b, h, i: (0, 0)),
                  pl.BlockSpec(g_out.shape, lambda b, h, i: (0, 0))],
        out_specs=pl.BlockSpec((DIFF_V, tq), lambda b, h, i: (h, b * nq + i)),
        out_shape=jax.ShapeDtypeStruct((DIFF_HEADS * DIFF_V, t), BF16),
        scratch_shapes=_stream_scratch(DIFF_V, tq, tk) + _stream_scratch(DIFF_V, tq, tk),
        compiler_params=_params("parallel", "parallel", "parallel"),
        name="diff_attn",
    )(dq_t, dq_t, dk, dk, y_t, lam_vec, g_out)


def _top16_rows(a):
    vals = []
    for _ in range(PEER_TOPK):
        mx = jnp.max(a, axis=0, keepdims=True)
        vals.append(mx)
        a = jnp.where(a >= mx, NEG_BIG, a)
    return vals


def _peer_route_kernel(qp_ref, keys_ref, a1_ref, a2_ref, thr_ref):
    tt = qp_ref.shape[1]
    rows = lax.broadcasted_iota(jnp.int32, (PEER_TOPK, tt), 0)

    def head(h, carry):
        off = pl.multiple_of(h * 2 * N_KEYS, 2 * N_KEYS)
        sc1 = jnp.dot(keys_ref[h, 0], qp_ref[pl.ds(off, N_KEYS), :], preferred_element_type=F32)
        sc2 = jnp.dot(keys_ref[h, 1], qp_ref[pl.ds(off + N_KEYS, N_KEYS), :], preferred_element_type=F32)
        a1 = (sc1 - jnp.max(sc1, axis=0, keepdims=True)) * LOG2E
        a2 = (sc2 - jnp.max(sc2, axis=0, keepdims=True)) * LOG2E
        v1 = _top16_rows(a1)
        v2 = _top16_rows(a2)
        v2m = jnp.zeros((PEER_TOPK, tt), F32)
        for j in range(PEER_TOPK):
            v2m = jnp.where(rows == j, v2[j], v2m)
        cand = jnp.concatenate([v1[i] + v2m for i in range(PEER_TOPK)], axis=0)
        tops = _top16_rows(cand)
        z = tops[0] * 0.0
        for tv in tops:
            z = z + jnp.exp2(tv)
        lz = jnp.log2(z)
        cand_z = jnp.concatenate([(v1[i] - lz) + v2m for i in range(PEER_TOPK)], axis=0)
        thr = jnp.min(jnp.where(cand >= tops[-1], cand_z, POS_BIG), axis=0, keepdims=True)
        a1_ref[h] = a1 - lz
        a2_ref[h] = a2
        thr_ref[h] = thr
        return carry

    lax.fori_loop(0, PEER_HEADS, head, 0)


def _peer_route(qp_t, keys, tt):
    t = qp_t.shape[1]
    return pl.pallas_call(
        _peer_route_kernel,
        grid=(t // tt,),
        in_specs=[pl.BlockSpec((qp_t.shape[0], tt), lambda i: (0, i)),
                  pl.BlockSpec(keys.shape, lambda i: (0, 0, 0, 0))],
        out_specs=[pl.BlockSpec((PEER_HEADS, N_KEYS, tt), lambda i: (0, 0, i)),
                   pl.BlockSpec((PEER_HEADS, N_KEYS, tt), lambda i: (0, 0, i)),
                   pl.BlockSpec((PEER_HEADS, 1, tt), lambda i: (0, 0, i))],
        out_shape=[jax.ShapeDtypeStruct((PEER_HEADS, N_KEYS, t), F32),
                   jax.ShapeDtypeStruct((PEER_HEADS, N_KEYS, t), F32),
                   jax.ShapeDtypeStruct((PEER_HEADS, 1, t), F32)],
        compiler_params=_params("parallel"),
        name="peer_route",
    )(qp_t, keys)


def _peer_dense_kernel(x_ref, u_ref, vt_ref, a1_ref, a2_ref, thr_ref, res_hbm, o_ref,
                       w0_ref, w1_ref, sem, *, te, n_tiles):
    i = pl.program_id(0)
    s = pl.program_id(1)
    tt = x_ref.shape[1]
    n_e = te // N_KEYS
    rows = o_ref.shape[0] // n_e
    tile = jnp.minimum(s, n_tiles - 1)

    def step(w_dst, w_src):
        for e in range(n_e):
            es = slice(e * N_KEYS, (e + 1) * N_KEYS)
            rs = slice(e * rows, (e + 1) * rows)
            if w_dst is not None:
                hh = jnp.dot(u_ref[es, :], x_ref[...], preferred_element_type=F32)
            if w_src is not None:
                o_ref[rs, :] += jnp.dot(vt_ref[rs, :], w_src[...], preferred_element_type=F32)
            if w_dst is not None:
                e1 = tile * n_e + e
                gate = jnp.zeros((N_KEYS, tt), F32)
                for h in range(PEER_HEADS):
                    sc = a2_ref[h] + a1_ref[h, pl.ds(e1, 1), :]
                    gate = gate + jnp.where(sc >= thr_ref[h], jnp.exp2(sc), 0.0)
                act = 0.5 * hh * (1.0 + lax.erf(hh * (2.0 ** -0.5)))
                w_dst[es, :] = (gate * act).astype(BF16)

    @pl.when(s == 0)
    def _():
        res = pltpu.make_async_copy(res_hbm.at[:, pl.ds(pl.multiple_of(i * tt, tt), tt)], o_ref, sem)
        res.start()
        step(w0_ref, None)
        res.wait()

    inner = jnp.logical_and(s > 0, s < n_tiles)
    pl.when(jnp.logical_and(inner, s % 2 == 0))(functools.partial(step, w0_ref, w1_ref))
    pl.when(jnp.logical_and(inner, s % 2 == 1))(functools.partial(step, w1_ref, w0_ref))
    pl.when(s == n_tiles)(functools.partial(step, None, (w0_ref, w1_ref)[(n_tiles - 1) % 2]))


def _peer_dense(xn_t, u, v_t, a1, a2, thr, res_t, tt, te):
    d, t = xn_t.shape
    n_tiles = N_EXPERTS // te
    once = dict(pipeline_mode=pl.Buffered(1))
    return pl.pallas_call(
        functools.partial(_peer_dense_kernel, te=te, n_tiles=n_tiles),
        grid=(t // tt, n_tiles + 1),
        in_specs=[pl.BlockSpec((d, tt), lambda i, s: (0, i), **once),
                  pl.BlockSpec((te, d), lambda i, s: (jnp.minimum(s, n_tiles - 1), 0)),
                  pl.BlockSpec((d, te), lambda i, s: (0, jnp.maximum(s - 1, 0))),
                  pl.BlockSpec((PEER_HEADS, N_KEYS, tt), lambda i, s: (0, 0, i), **once),
                  pl.BlockSpec((PEER_HEADS, N_KEYS, tt), lambda i, s: (0, 0, i), **once),
                  pl.BlockSpec((PEER_HEADS, 1, tt), lambda i, s: (0, 0, i), **once),
                  pl.BlockSpec(memory_space=pl.ANY)],
        out_specs=pl.BlockSpec((d, tt), lambda i, s: (0, i)),
        out_shape=jax.ShapeDtypeStruct((d, t), F32),
        scratch_shapes=[pltpu.VMEM((te, tt), BF16), pltpu.VMEM((te, tt), BF16), pltpu.SemaphoreType.DMA(())],
        compiler_params=_params("parallel", "arbitrary"),
        name="peer_dense",
    )(xn_t, u, v_t, a1, a2, thr, res_t)


def _rep(g):
    return jnp.broadcast_to(g.astype(F32)[:, None], (g.shape[0], LANES))


def _rope_tables(seq, dim, theta):
    inv = theta ** (-jnp.arange(0, dim, 2, dtype=F32) / dim)
    ang = inv[:, None] * jnp.arange(seq, dtype=F32)[None, :]
    return jnp.cos(ang), jnp.sin(ang)


def _tile(n, pref):
    return pref if n % pref == 0 else n


def _trunk_t(x_t, batch, seq, p):
    t = x_t.shape[1]
    tt = _tile(t, 512)
    tp = _tile(seq, 256)
    tq = _tile(seq, 512)
    tk = 512 if seq % 1024 == 0 else seq // 2
    cm, sm = _rope_tables(seq, MLA_ROPE, MLA_ROPE_THETA)
    cd, sd = _rope_tables(seq, ROT_DIM, ROPE_THETA)
    depth = p["w_in"].shape[0]
    for l in range(depth):
        w_in = p["w_in"][l]
        w_in_t = jnp.concatenate([w_in[:, IN_WIDTH - DV_W:], w_in[:, :IN_WIDTH - DV_W]], axis=1).T.astype(BF16)
        wq_t = p["w_q_b"][l].T.astype(BF16)
        wkv_t = p["w_kv_b"][l].T.astype(BF16)
        wo_t = p["w_out"][l].T.astype(BF16)
        wpq_t = p["w_peer_q"][l].T.astype(BF16)
        keys = p["peer_keys"][l].astype(BF16)
        u = p["peer_u"][l].astype(BF16)
        v_t = p["peer_v"][l].T.astype(BF16)
        gains = [_rep(p[k][l]) for k in ("q_a_norm", "kv_a_norm", "mla_q_norm", "mla_k_norm",
                                         "diff_q_norm", "diff_k_norm")]
        lam_vec = jnp.stack([p[k][l].astype(F32) for k in ("lambda_q1", "lambda_k1", "lambda_q2", "lambda_k2")])
        lam_init = 0.8 - 0.6 * math.exp(-0.3 * l)

        xn = _norm_t(x_t, _rep(p["attn_norm"][l]), tt)
        y_t = _matmul_t([(w_in_t, xn)], None, BF16, tt, 576, "in_proj")
        q_t, k, vm_t, dq_t, dk = _prep(y_t, wq_t, wkv_t, gains, (cm, sm, cd, sd), seq, tp)
        o_mla = _mla_attn(q_t, k, vm_t, batch, seq, tq, tk)
        o_diff = _diff_attn(dq_t, dk, y_t, lam_vec, _rep(p["diff_out_norm"][l]), batch, seq, tq, tk, lam_init)
        nm = MLA_HEADS * MLA_V
        x_t = _matmul_t([(wo_t[:, :nm], o_mla), (wo_t[:, nm:], o_diff)], x_t, F32, tt, 512, "out_proj")

        xn2 = _norm_t(x_t, _rep(p["ffn_norm"][l]), tt)
        qp_t = _matmul_t([(wpq_t, xn2)], None, BF16, tt, 512, "peer_query")
        a1, a2, thr = _peer_route(qp_t, keys, _tile(t, 256))
        x_t = _peer_dense(xn2, u, v_t, a1, a2, thr, x_t, tt, 512)
    return x_t


def kernel(x_prompt, x_sample, attn_norm, w_in, q_a_norm, w_q_b, kv_a_norm, w_kv_b, mla_q_norm, mla_k_norm,
           diff_q_norm, diff_k_norm, lambda_q1, lambda_k1, lambda_q2, lambda_k2, diff_out_norm, w_out,
           ffn_norm, w_peer_q, peer_keys, peer_u, peer_v):
    p = dict(attn_norm=attn_norm, w_in=w_in, q_a_norm=q_a_norm, w_q_b=w_q_b, kv_a_norm=kv_a_norm,
             w_kv_b=w_kv_b, mla_q_norm=mla_q_norm, mla_k_norm=mla_k_norm, diff_q_norm=diff_q_norm,
             diff_k_norm=diff_k_norm, lambda_q1=lambda_q1, lambda_k1=lambda_k1, lambda_q2=lambda_q2,
             lambda_k2=lambda_k2, diff_out_norm=diff_out_norm, w_out=w_out, ffn_norm=ffn_norm,
             w_peer_q=w_peer_q, peer_keys=peer_keys, peer_u=peer_u, peer_v=peer_v)
    bp, seq, d = x_prompt.shape
    bs = x_sample.shape[0]
    assert x_sample.shape[1] == seq
    x = jnp.concatenate([x_prompt.reshape(bp * seq, d), x_sample.reshape(bs * seq, d)], axis=0)
    y = _trunk_t(x.T, bp + bs, seq, p).T
    return y[:bp * seq].reshape(bp, seq, d), y[bp * seq:].reshape(bs, seq, d)
```

```python
import functools
import math

import jax
import jax.numpy as jnp
from jax import lax
from jax.experimental import pallas as pl
from jax.experimental.pallas import tpu as pltpu

F32 = jnp.float32
BF16 = jnp.bfloat16

D_MODEL = 4096
MLA_HEADS = 16
MLA_NOPE = 128
MLA_ROPE = 64
MLA_QK = MLA_NOPE + MLA_ROPE
MLA_V = 128
MLA_PAD = 256
Q_LORA = 768
KV_LORA = 512
MLA_ROPE_THETA = 10000.0
DIFF_HEADS = 8
DIFF_HD = 128
DIFF_V = 2 * DIFF_HD
ROT_DIM = DIFF_HD // 4
ROPE_THETA = 500000.0
PEER_HEADS = 8
N_KEYS = 128
N_EXPERTS = N_KEYS * N_KEYS
PEER_TOPK = 16
EPS = 1e-6

DV_W = DIFF_HEADS * DIFF_V
DQ_W = DIFF_HEADS * 2 * DIFF_HD
OFF_DV = 0
OFF_QA = OFF_DV + DV_W
OFF_KVA = OFF_QA + Q_LORA
OFF_KR = OFF_KVA + KV_LORA
OFF_DQ = OFF_KR + MLA_ROPE
OFF_DK = OFF_DQ + DQ_W
IN_WIDTH = OFF_DK + DQ_W

LANES = 128
VMEM_LIMIT_BYTES = 56 * 1024 * 1024
LOG2E = 1.4426950408889634
NEG_BIG = -1e30
POS_BIG = 1e30
PEER_TE = 512
NRM_ROWS = 8
NORM_SLACK = 1.01
MAX_SAFE_SPAN = 100.0


def _params(*sem, flags=None):
    return pltpu.CompilerParams(dimension_semantics=sem, vmem_limit_bytes=VMEM_LIMIT_BYTES, flags=flags)


def _lane_tile(g, width):
    reps = width // LANES
    return g if reps == 1 else jnp.concatenate([g] * reps, axis=1)


def _rms_rows(x, g, n):
    r = lax.rsqrt(jnp.sum(x * x, axis=0, keepdims=True) * (1.0 / n) + EPS)
    return x * r * g


def _norm_t_kernel(x_ref, g_ref, o_ref):
    x = x_ref[...]
    r = lax.rsqrt(jnp.sum(x * x, axis=0, keepdims=True) * (1.0 / x.shape[0]) + EPS)
    g = g_ref[...]
    for c in range(x.shape[1] // LANES):
        sl = slice(c * LANES, (c + 1) * LANES)
        o_ref[:, sl] = (x[:, sl] * r[:, sl] * g).astype(o_ref.dtype)


def _norm_t(x_t, g_rep, tt):
    d, t = x_t.shape
    return pl.pallas_call(
        _norm_t_kernel,
        grid=(t // tt,),
        in_specs=[pl.BlockSpec((d, tt), lambda i: (0, i)),
                  pl.BlockSpec((d, LANES), lambda i: (0, 0))],
        out_specs=pl.BlockSpec((d, tt), lambda i: (0, i)),
        out_shape=jax.ShapeDtypeStruct((d, t), BF16),
        compiler_params=_params("parallel"),
        name="norm_t",
    )(x_t, g_rep)


def _matmul_t_kernel(*refs, n_pairs, has_res):
    o_ref = refs[-1]
    acc = None
    for p in range(n_pairs):
        d = jnp.dot(refs[2 * p][...], refs[2 * p + 1][...], preferred_element_type=F32)
        acc = d if acc is None else acc + d
    if has_res:
        acc = refs[2 * n_pairs][...] + acc
    o_ref[...] = acc.astype(o_ref.dtype)


def _matmul_t(pairs, res, out_dtype, tt, tn, name):
    n = pairs[0][0].shape[0]
    t = pairs[0][1].shape[1]
    in_specs, args = [], []
    for w, a in pairs:
        in_specs.append(pl.BlockSpec((tn, w.shape[1]), lambda i, j: (j, 0)))
        in_specs.append(pl.BlockSpec((a.shape[0], tt), lambda i, j: (0, i)))
        args += [w, a]
    if res is not None:
        in_specs.append(pl.BlockSpec((tn, tt), lambda i, j: (j, i)))
        args.append(res)
    return pl.pallas_call(
        functools.partial(_matmul_t_kernel, n_pairs=len(pairs), has_res=res is not None),
        grid=(t // tt, n // tn),
        in_specs=in_specs,
        out_specs=pl.BlockSpec((tn, tt), lambda i, j: (j, i)),
        out_shape=jax.ShapeDtypeStruct((n, t), out_dtype),
        compiler_params=_params("parallel", "parallel"),
        name=name,
    )(*args)


def _rope_rows(x1, x2, cos, sin):
    return x1 * cos - x2 * sin, x2 * cos + x1 * sin


def _prep_kernel(y_ref, wq_ref, wkv_ref, gqa_ref, gkva_ref, gq_ref, gk_ref, gdq_ref, gdk_ref,
                 cm_ref, sm_ref, cd_ref, sd_ref,
                 q_ref, k_ref, v_ref, dq_ref, dk_ref, nrm_ref):
    tt = y_ref.shape[1]
    sumsq = lambda x: jnp.sum(x * x, axis=0, keepdims=True)
    q_ss = k_ss = dq_ss = dk_ss = jnp.zeros((1, tt), F32)
    cm, sm = cm_ref[...], sm_ref[...]
    cd, sd = cd_ref[...], sd_ref[...]
    half = MLA_ROPE // 2
    zpad = jnp.zeros((MLA_PAD - MLA_QK, tt), F32)
    q_scale = (MLA_QK ** -0.5) * LOG2E
    d_scale = (DIFF_HD ** -0.5) * LOG2E

    qa = y_ref[OFF_QA:OFF_QA + Q_LORA, :].astype(F32)
    qan = _rms_rows(qa, _lane_tile(gqa_ref[...], tt), Q_LORA).astype(BF16)
    q = jnp.dot(wq_ref[...], qan, preferred_element_type=F32)
    gq = _lane_tile(gq_ref[...], tt)
    for h in range(MLA_HEADS):
        qn = _rms_rows(q[h * MLA_QK:(h + 1) * MLA_QK], gq, MLA_QK)
        o1, o2 = _rope_rows(qn[MLA_NOPE:MLA_NOPE + half], qn[MLA_NOPE + half:], cm, sm)
        qfull = jnp.concatenate([qn[:MLA_NOPE], o1, o2, zpad], axis=0) * q_scale
        q_ss = jnp.maximum(q_ss, sumsq(qfull))
        q_ref[h] = qfull.astype(BF16)

    kva = y_ref[OFF_KVA:OFF_KVA + KV_LORA, :].astype(F32)
    kvan = _rms_rows(kva, _lane_tile(gkva_ref[...], tt), KV_LORA).astype(BF16)
    kv = jnp.dot(wkv_ref[...], kvan, preferred_element_type=F32)
    kr = y_ref[OFF_KR:OFF_KR + MLA_ROPE, :].astype(F32)
    kr_ss = jnp.sum(kr * kr, axis=0, keepdims=True)
    gk = _lane_tile(gk_ref[...], tt)
    krg = kr * gk[MLA_NOPE:]
    kr1, kr2 = _rope_rows(krg[:half], krg[half:], cm, sm)
    for h in range(MLA_HEADS):
        base = h * (MLA_NOPE + MLA_V)
        kn = kv[base:base + MLA_NOPE]
        r = lax.rsqrt((jnp.sum(kn * kn, axis=0, keepdims=True) + kr_ss) * (1.0 / MLA_QK) + EPS)
        kfull = jnp.concatenate([kn * r * gk[:MLA_NOPE], kr1 * r, kr2 * r, zpad], axis=0)
        k_ss = jnp.maximum(k_ss, sumsq(kfull))
        k_ref[h] = jnp.transpose(kfull).astype(BF16)
        v_ref[h] = kv[base + MLA_NOPE:base + MLA_NOPE + MLA_V].astype(BF16)

    gdq = _lane_tile(gdq_ref[...], tt)
    gdk = _lane_tile(gdk_ref[...], tt)
    hr = ROT_DIM // 2
    for j in range(2 * DIFF_HEADS):
        xq = _rms_rows(y_ref[OFF_DQ + j * DIFF_HD:OFF_DQ + (j + 1) * DIFF_HD, :].astype(F32), gdq, DIFF_HD)
        o1, o2 = _rope_rows(xq[:hr], xq[hr:ROT_DIM], cd, sd)
        dqfull = jnp.concatenate([o1, o2, xq[ROT_DIM:]], axis=0) * d_scale
        dq_ss = jnp.maximum(dq_ss, sumsq(dqfull))
        dq_ref[j] = dqfull.astype(BF16)
        xk = _rms_rows(y_ref[OFF_DK + j * DIFF_HD:OFF_DK + (j + 1) * DIFF_HD, :].astype(F32), gdk, DIFF_HD)
        o1, o2 = _rope_rows(xk[:hr], xk[hr:ROT_DIM], cd, sd)
        dkfull = jnp.concatenate([o1, o2, xk[ROT_DIM:]], axis=0)
        dk_ss = jnp.maximum(dk_ss, sumsq(dkfull))
        dk_ref[j] = jnp.transpose(dkfull).astype(BF16)
    row = lax.broadcasted_iota(jnp.int32, nrm_ref.shape, 0)
    nrm_ref[...] = jnp.where(row == 0, q_ss, jnp.where(row == 1, k_ss, jnp.where(row == 2, dq_ss, dk_ss)))


def _prep(y_t, wq_t, wkv_t, gains, tables, seq, tt):
    t = y_t.shape[1]
    nseq = seq // tt
    full = lambda a: pl.BlockSpec(a.shape, lambda i: (0,) * a.ndim)
    tab = lambda a: pl.BlockSpec((a.shape[0], tt), lambda i: (0, i % nseq))
    nh, nd = MLA_HEADS, 2 * DIFF_HEADS
    return pl.pallas_call(
        _prep_kernel,
        grid=(t // tt,),
        in_specs=[pl.BlockSpec((IN_WIDTH, tt), lambda i: (0, i)), full(wq_t), full(wkv_t)]
                 + [full(g) for g in gains] + [tab(a) for a in tables],
        out_specs=[pl.BlockSpec((nh, MLA_PAD, tt), lambda i: (0, 0, i)),
                   pl.BlockSpec((nh, tt, MLA_PAD), lambda i: (0, i, 0)),
                   pl.BlockSpec((nh, MLA_V, tt), lambda i: (0, 0, i)),
                   pl.BlockSpec((nd, DIFF_HD, tt), lambda i: (0, 0, i)),
                   pl.BlockSpec((nd, tt, DIFF_HD), lambda i: (0, i, 0)),
                   pl.BlockSpec((NRM_ROWS, tt), lambda i: (0, i))],
        out_shape=[jax.ShapeDtypeStruct((nh, MLA_PAD, t), BF16),
                   jax.ShapeDtypeStruct((nh, t, MLA_PAD), BF16),
                   jax.ShapeDtypeStruct((nh, MLA_V, t), BF16),
                   jax.ShapeDtypeStruct((nd, DIFF_HD, t), BF16),
                   jax.ShapeDtypeStruct((nd, t, DIFF_HD), BF16),
                   jax.ShapeDtypeStruct((NRM_ROWS, t), F32)],
        compiler_params=_params("parallel"),
        name="qkv_prep",
    )(y_t, wq_t, wkv_t, *gains, *tables)


class _Stream:
    def __init__(self, q, k_ref, bound, s_ref, p_ref, a_ref, m_ref, l_ref, acc_ref):
        self.q, self.k_ref, self.bound = q, k_ref, bound
        self.s_ref, self.p_ref, self.a_ref = s_ref, p_ref, a_ref
        self.m_ref, self.l_ref, self.acc_ref = m_ref, l_ref, acc_ref

    def init(self):
        self.m_ref[...] = jnp.full(self.m_ref.shape, NEG_BIG, F32) if self.bound is None else self.bound
        self.l_ref[...] = jnp.zeros(self.l_ref.shape, F32)
        self.acc_ref[...] = jnp.zeros(self.acc_ref.shape, F32)

    def qk(self, off, tk, slot):
        self.s_ref[slot] = jnp.dot(self.k_ref[0, pl.ds(off, tk), :], self.q, preferred_element_type=F32)

    def softmax(self, slot):
        s = self.s_ref[slot]
        if self.bound is not None:
            p = jnp.exp2(s - self.m_ref[...])
            self.l_ref[...] += jnp.sum(p, axis=0, keepdims=True)
            self.p_ref[slot] = p.astype(BF16)
            return
        m_prev = self.m_ref[...]
        m_new = jnp.maximum(m_prev, jnp.max(s, axis=0, keepdims=True))
        p = jnp.exp2(s - m_new)
        alpha = jnp.exp2(m_prev - m_new)
        self.l_ref[...] = alpha * self.l_ref[...] + jnp.sum(p, axis=0, keepdims=True)
        self.m_ref[...] = m_new
        self.a_ref[slot] = alpha
        self.p_ref[slot] = p.astype(BF16)

    def pv(self, vc, slot):
        pv = jnp.dot(vc, self.p_ref[slot], preferred_element_type=F32)
        if self.bound is not None:
            self.acc_ref[...] += pv
        else:
            self.acc_ref[...] = self.a_ref[slot] * self.acc_ref[...] + pv

    def result(self):
        return self.acc_ref[...] / self.l_ref[...]


def _attend(streams, v_chunk, n, tk, unroll):
    def offset(i):
        return i * tk if isinstance(i, int) else pl.multiple_of(i * tk, tk)

    def qk(i, slot):
        for st in streams:
            st.qk(offset(i), tk, slot)

    def pv(i, slot):
        vc = v_chunk(offset(i))
        for st in streams:
            st.pv(vc, slot)

    def softmax(slot):
        for st in streams:
            st.softmax(slot)

    for st in streams:
        st.init()
    qk(0, 0)
    qk(1, 1)
    softmax(0)

    def pair(j, carry):
        i = 2 * j + 1
        qk(i + 1, 0)
        pv(i - 1, 0)
        softmax(1)
        qk(i + 2, 1)
        pv(i, 1)
        softmax(0)
        return carry

    if unroll:
        for j in range((n - 2) // 2):
            pair(j, 0)
    else:
        lax.fori_loop(0, (n - 2) // 2, pair, 0)
    pv(n - 2, 0)
    softmax(1)
    pv(n - 1, 1)


def _stream_scratch(rows_v, tq, tk):
    return [pltpu.VMEM((2, tk, tq), F32), pltpu.VMEM((2, tk, tq), BF16), pltpu.VMEM((2, 1, tq), F32),
            pltpu.VMEM((1, tq), F32), pltpu.VMEM((1, tq), F32), pltpu.VMEM((rows_v, tq), F32)]


def _score_bound(qs, kb_ref, fast):
    if not fast:
        return None
    ss = None
    for q in qs:
        qf = q.astype(F32)
        v = jnp.sum(qf * qf, axis=0, keepdims=True)
        ss = v if ss is None else jnp.maximum(ss, v)
    return jnp.sqrt(ss) * _lane_tile(kb_ref[...], ss.shape[1])


def _mla_attn_kernel(q_ref, k_ref, v_ref, kb_ref, o_ref, *scratch, tk, fast):
    q = q_ref[0]
    st = _Stream(q, k_ref, _score_bound([q], kb_ref, fast), *scratch)
    _attend([st], lambda off: v_ref[0, :, pl.ds(off, tk)], k_ref.shape[1] // tk, tk, unroll=fast)
    o_ref[...] = st.result().astype(o_ref.dtype)


def _mla_attn(q_t, k, v_t, kb, batch, seq, tq, tk, fast):
    nh, _, t = q_t.shape
    nq = seq // tq
    return pl.pallas_call(
        functools.partial(_mla_attn_kernel, tk=tk, fast=fast),
        grid=(batch, nh, nq),
        in_specs=[pl.BlockSpec((1, MLA_PAD, tq), lambda b, h, i: (h, 0, b * nq + i)),
                  pl.BlockSpec((1, seq, MLA_PAD), lambda b, h, i: (h, b, 0)),
                  pl.BlockSpec((1, MLA_V, seq), lambda b, h, i: (h, 0, b)),
                  pl.BlockSpec(kb.shape, lambda b, h, i: (0, 0))],
        out_specs=pl.BlockSpec((MLA_V, tq), lambda b, h, i: (h, b * nq + i)),
        out_shape=jax.ShapeDtypeStruct((nh * MLA_V, t), BF16),
        scratch_shapes=_stream_scratch(MLA_V, tq, tk),
        compiler_params=_params("parallel", "parallel", "parallel"),
        name="mla_attn_fast" if fast else "mla_attn",
    )(q_t, k, v_t, kb)


def _diff_attn_kernel(q1_ref, q2_ref, k1_ref, k2_ref, v_ref, lam_ref, g_ref, kb_ref, o_ref, *scratch,
                      tk, lam_init, fast):
    q1, q2 = q1_ref[0], q2_ref[0]
    bound = _score_bound([q1, q2], kb_ref, fast)
    st1 = _Stream(q1, k1_ref, bound, *scratch[:6])
    st2 = _Stream(q2, k2_ref, bound, *scratch[6:])
    _attend([st1, st2], lambda off: v_ref[:, pl.ds(off, tk)], k1_ref.shape[1] // tk, tk, unroll=fast)
    lv = lam_ref[...]
    lam = (jnp.exp(jnp.sum(lv[0:1] * lv[1:2], axis=1, keepdims=True))
           - jnp.exp(jnp.sum(lv[2:3] * lv[3:4], axis=1, keepdims=True)) + lam_init)
    o = st1.result() - lam * st2.result()
    g = _lane_tile(g_ref[...], o.shape[1])
    o_ref[...] = (_rms_rows(o, g, DIFF_V) * (1.0 - lam_init)).astype(o_ref.dtype)


def _diff_attn(dq_t, dk, y_t, lam_vec, g_out, kb, batch, seq, tq, tk, lam_init, fast):
    t = dq_t.shape[2]
    nq = seq // tq
    return pl.pallas_call(
        functools.partial(_diff_attn_kernel, tk=tk, lam_init=lam_init, fast=fast),
        grid=(batch, DIFF_HEADS, nq),
        in_specs=[pl.BlockSpec((1, DIFF_HD, tq), lambda b, h, i: (2 * h, 0, b * nq + i)),
                  pl.BlockSpec((1, DIFF_HD, tq), lambda b, h, i: (2 * h + 1, 0, b * nq + i)),
                  pl.BlockSpec((1, seq, DIFF_HD), lambda b, h, i: (2 * h, b, 0)),
                  pl.BlockSpec((1, seq, DIFF_HD), lambda b, h, i: (2 * h + 1, b, 0)),
                  pl.BlockSpec((DIFF_V, seq), lambda b, h, i: (h, b)),
                  pl.BlockSpec(lam_vec.shape, lambda b, h, i: (0, 0)),
                  pl.BlockSpec(g_out.shape, lambda b, h, i: (0, 0)),
                  pl.BlockSpec(kb.shape, lambda b, h, i: (0, 0))],
        out_specs=pl.BlockSpec((DIFF_V, tq), lambda b, h, i: (h, b * nq + i)),
        out_shape=jax.ShapeDtypeStruct((DIFF_HEADS * DIFF_V, t), BF16),
        scratch_shapes=_stream_scratch(DIFF_V, tq, tk) + _stream_scratch(DIFF_V, tq, tk),
        compiler_params=_params("parallel", "parallel", "parallel"),
        name="diff_attn_fast" if fast else "diff_attn",
    )(dq_t, dq_t, dk, dk, y_t, lam_vec, g_out, kb)


def _top16_rows(a):
    vals = []
    for _ in range(PEER_TOPK):
        mx = jnp.max(a, axis=0, keepdims=True)
        vals.append(mx)
        a = jnp.where(a >= mx, NEG_BIG, a)
    return vals


def _peer_route_kernel(qp_ref, keys_ref, a1_ref, a2_ref, thr_ref):
    tt = qp_ref.shape[1]
    rows = lax.broadcasted_iota(jnp.int32, (PEER_TOPK, tt), 0)

    def head(h, carry):
        off = pl.multiple_of(h * 2 * N_KEYS, 2 * N_KEYS)
        sc1 = jnp.dot(keys_ref[h, 0], qp_ref[pl.ds(off, N_KEYS), :], preferred_element_type=F32)
        sc2 = jnp.dot(keys_ref[h, 1], qp_ref[pl.ds(off + N_KEYS, N_KEYS), :], preferred_element_type=F32)
        a1 = (sc1 - jnp.max(sc1, axis=0, keepdims=True)) * LOG2E
        a2 = (sc2 - jnp.max(sc2, axis=0, keepdims=True)) * LOG2E
        v1 = _top16_rows(a1)
        v2 = _top16_rows(a2)
        v2m = jnp.zeros((PEER_TOPK, tt), F32)
        for j in range(PEER_TOPK):
            v2m = jnp.where(rows == j, v2[j], v2m)
        cand = jnp.concatenate([v1[i] + v2m for i in range(PEER_TOPK)], axis=0)
        tops = _top16_rows(cand)
        z = tops[0] * 0.0
        for tv in tops:
            z = z + jnp.exp2(tv)
        lz = jnp.log2(z)
        cand_z = jnp.concatenate([(v1[i] - lz) + v2m for i in range(PEER_TOPK)], axis=0)
        thr = jnp.min(jnp.where(cand >= tops[-1], cand_z, POS_BIG), axis=0, keepdims=True)
        a1_ref[h] = a1 - lz
        a2_ref[h] = a2
        thr_ref[h] = thr
        return carry

    lax.fori_loop(0, PEER_HEADS, head, 0)


def _peer_route(qp_t, keys, tt):
    t = qp_t.shape[1]
    return pl.pallas_call(
        _peer_route_kernel,
        grid=(t // tt,),
        in_specs=[pl.BlockSpec((qp_t.shape[0], tt), lambda i: (0, i)),
                  pl.BlockSpec(keys.shape, lambda i: (0, 0, 0, 0))],
        out_specs=[pl.BlockSpec((PEER_HEADS, N_KEYS, tt), lambda i: (0, 0, i)),
                   pl.BlockSpec((PEER_HEADS, N_KEYS, tt), lambda i: (0, 0, i)),
                   pl.BlockSpec((PEER_HEADS, 1, tt), lambda i: (0, 0, i))],
        out_shape=[jax.ShapeDtypeStruct((PEER_HEADS, N_KEYS, t), F32),
                   jax.ShapeDtypeStruct((PEER_HEADS, N_KEYS, t), F32),
                   jax.ShapeDtypeStruct((PEER_HEADS, 1, t), F32)],
        compiler_params=_params("parallel"),
        name="peer_route",
    )(qp_t, keys)


def _peer_dense_kernel(x_ref, u_ref, vt_ref, a1_ref, a2_ref, thr_ref, res_hbm, o_ref,
                       w0_ref, w1_ref, sem, *, te, n_tiles):
    i = pl.program_id(0)
    s = pl.program_id(1)
    tt = x_ref.shape[1]
    n_e = te // N_KEYS
    rows = o_ref.shape[0] // n_e
    tile = jnp.minimum(s, n_tiles - 1)

    def step(w_dst, w_src):
        for e in range(n_e):
            es = slice(e * N_KEYS, (e + 1) * N_KEYS)
            rs = slice(e * rows, (e + 1) * rows)
            if w_dst is not None:
                hh = jnp.dot(u_ref[es, :], x_ref[...], preferred_element_type=F32)
            if w_src is not None:
                o_ref[rs, :] += jnp.dot(vt_ref[0, rs, :], w_src[...], preferred_element_type=F32)
            if w_dst is not None:
                e1 = tile * n_e + e
                gate = jnp.zeros((N_KEYS, tt), F32)
                for h in range(PEER_HEADS):
                    sc = a2_ref[h] + a1_ref[h, pl.ds(e1, 1), :]
                    gate = gate + jnp.where(sc >= thr_ref[h], jnp.exp2(sc), 0.0)
                act = 0.5 * hh * (1.0 + lax.erf(hh * (2.0 ** -0.5)))
                w_dst[es, :] = (gate * act).astype(BF16)

    @pl.when(s == 0)
    def _():
        res = pltpu.make_async_copy(res_hbm.at[:, pl.ds(pl.multiple_of(i * tt, tt), tt)], o_ref, sem)
        res.start()
        step(w0_ref, None)
        res.wait()

    inner = jnp.logical_and(s > 0, s < n_tiles)
    pl.when(jnp.logical_and(inner, s % 2 == 0))(functools.partial(step, w0_ref, w1_ref))
    pl.when(jnp.logical_and(inner, s % 2 == 1))(functools.partial(step, w1_ref, w0_ref))
    pl.when(s == n_tiles)(functools.partial(step, None, (w0_ref, w1_ref)[(n_tiles - 1) % 2]))


def _peer_dense(xn_t, u, v_t, a1, a2, thr, res_t, tt, te):
    d, t = xn_t.shape
    n_tiles = N_EXPERTS // te
    once = dict(pipeline_mode=pl.Buffered(1))
    return pl.pallas_call(
        functools.partial(_peer_dense_kernel, te=te, n_tiles=n_tiles),
        grid=(t // tt, n_tiles + 1),
        in_specs=[pl.BlockSpec((d, tt), lambda i, s: (0, i), **once),
                  pl.BlockSpec((te, d), lambda i, s: (jnp.minimum(s, n_tiles - 1), 0)),
                  pl.BlockSpec((1, d, te), lambda i, s: (jnp.maximum(s - 1, 0), 0, 0)),
                  pl.BlockSpec((PEER_HEADS, N_KEYS, tt), lambda i, s: (0, 0, i), **once),
                  pl.BlockSpec((PEER_HEADS, N_KEYS, tt), lambda i, s: (0, 0, i), **once),
                  pl.BlockSpec((PEER_HEADS, 1, tt), lambda i, s: (0, 0, i), **once),
                  pl.BlockSpec(memory_space=pl.ANY)],
        out_specs=pl.BlockSpec((d, tt), lambda i, s: (0, i)),
        out_shape=jax.ShapeDtypeStruct((d, t), F32),
        scratch_shapes=[pltpu.VMEM((te, tt), BF16), pltpu.VMEM((te, tt), BF16), pltpu.SemaphoreType.DMA(())],
        compiler_params=_params("parallel", "arbitrary"),
        name="peer_dense",
    )(xn_t, u, v_t, a1, a2, thr, res_t)


def _rep(g):
    return jnp.broadcast_to(g.astype(F32)[:, None], (g.shape[0], LANES))


def _rope_tables(seq, dim, theta):
    inv = theta ** (-jnp.arange(0, dim, 2, dtype=F32) / dim)
    ang = inv[:, None] * jnp.arange(seq, dtype=F32)[None, :]
    return jnp.cos(ang), jnp.sin(ang)


def _tile(n, pref):
    return pref if n % pref == 0 else n


def _trunk_t(x_t, batch, seq, p):
    t = x_t.shape[1]
    tt = _tile(t, 512)
    tp = _tile(seq, 256)
    tq = _tile(seq, 512)
    tk = 512 if seq % 1024 == 0 else seq // 2
    cm, sm = _rope_tables(seq, MLA_ROPE, MLA_ROPE_THETA)
    cd, sd = _rope_tables(seq, ROT_DIM, ROPE_THETA)
    depth = p["w_in"].shape[0]
    for l in range(depth):
        w_in = p["w_in"][l]
        w_in_t = jnp.concatenate([w_in[:, IN_WIDTH - DV_W:], w_in[:, :IN_WIDTH - DV_W]], axis=1).T.astype(BF16)
        wq_t = p["w_q_b"][l].T.astype(BF16)
        wkv_t = p["w_kv_b"][l].T.astype(BF16)
        wo_t = p["w_out"][l].T.astype(BF16)
        wpq_t = p["w_peer_q"][l].T.astype(BF16)
        keys = p["peer_keys"][l].astype(BF16)
        u = p["peer_u"][l].astype(BF16)
        v_t = jnp.swapaxes(p["peer_v"][l].reshape(N_EXPERTS // PEER_TE, PEER_TE, -1), 1, 2).astype(BF16)
        gains = [_rep(p[k][l]) for k in ("q_a_norm", "kv_a_norm", "mla_q_norm", "mla_k_norm",
                                         "diff_q_norm", "diff_k_norm")]
        lam_vec = jnp.stack([p[k][l].astype(F32) for k in ("lambda_q1", "lambda_k1", "lambda_q2", "lambda_k2")])
        lam_init = 0.8 - 0.6 * math.exp(-0.3 * l)

        xn = _norm_t(x_t, _rep(p["attn_norm"][l]), tt)
        y_t = _matmul_t([(w_in_t, xn)], None, BF16, tt, 576, "in_proj")
        q_t, k, vm_t, dq_t, dk, nrm = _prep(y_t, wq_t, wkv_t, gains, (cm, sm, cd, sd), seq, tp)
        qn, kn, dqn, dkn = [jnp.sqrt(jnp.max(nrm[r])) * NORM_SLACK for r in range(4)]
        kb_mla = jnp.full((1, LANES), kn, F32)
        kb_diff = jnp.full((1, LANES), dkn, F32)
        g_out = _rep(p["diff_out_norm"][l])
        o_mla = lax.cond(2.0 * qn * kn <= MAX_SAFE_SPAN,
                         lambda: _mla_attn(q_t, k, vm_t, kb_mla, batch, seq, tq, tk, True),
                         lambda: _mla_attn(q_t, k, vm_t, kb_mla, batch, seq, tq, tk, False))
        o_diff = lax.cond(2.0 * dqn * dkn <= MAX_SAFE_SPAN,
                          lambda: _diff_attn(dq_t, dk, y_t, lam_vec, g_out, kb_diff, batch, seq, tq, tk, lam_init, True),
                          lambda: _diff_attn(dq_t, dk, y_t, lam_vec, g_out, kb_diff, batch, seq, tq, tk, lam_init, False))
        nm = MLA_HEADS * MLA_V
        x_t = _matmul_t([(wo_t[:, :nm], o_mla), (wo_t[:, nm:], o_diff)], x_t, F32, tt, 512, "out_proj")

        xn2 = _norm_t(x_t, _rep(p["ffn_norm"][l]), tt)
        qp_t = _matmul_t([(wpq_t, xn2)], None, BF16, tt, 512, "peer_query")
        a1, a2, thr = _peer_route(qp_t, keys, _tile(t, 256))
        x_t = _peer_dense(xn2, u, v_t, a1, a2, thr, x_t, tt, PEER_TE)
    return x_t


def kernel(x_prompt, x_sample, attn_norm, w_in, q_a_norm, w_q_b, kv_a_norm, w_kv_b, mla_q_norm, mla_k_norm,
           diff_q_norm, diff_k_norm, lambda_q1, lambda_k1, lambda_q2, lambda_k2, diff_out_norm, w_out,
           ffn_norm, w_peer_q, peer_keys, peer_u, peer_v):
    p = dict(attn_norm=attn_norm, w_in=w_in, q_a_norm=q_a_norm, w_q_b=w_q_b, kv_a_norm=kv_a_norm,
             w_kv_b=w_kv_b, mla_q_norm=mla_q_norm, mla_k_norm=mla_k_norm, diff_q_norm=diff_q_norm,
             diff_k_norm=diff_k_norm, lambda_q1=lambda_q1, lambda_k1=lambda_k1, lambda_q2=lambda_q2,
             lambda_k2=lambda_k2, diff_out_norm=diff_out_norm, w_out=w_out, ffn_norm=ffn_norm,
             w_peer_q=w_peer_q, peer_keys=peer_keys, peer_u=peer_u, peer_v=peer_v)
    bp, seq, d = x_prompt.shape
    bs = x_sample.shape[0]
    assert x_sample.shape[1] == seq
    x = jnp.concatenate([x_prompt.reshape(bp * seq, d), x_sample.reshape(bs * seq, d)], axis=0)
    y = _trunk_t(x.T, bp + bs, seq, p).T
    return y[:bp * seq].reshape(bp, seq, d), y[bp * seq:].reshape(bs, seq, d)
```

```python
import functools
import math

import jax
import jax.numpy as jnp
from jax import lax
from jax.experimental import pallas as pl
from jax.experimental.pallas import tpu as pltpu

F32 = jnp.float32
BF16 = jnp.bfloat16

D_MODEL = 4096
MLA_HEADS = 16
MLA_NOPE = 128
MLA_ROPE = 64
MLA_QK = MLA_NOPE + MLA_ROPE
MLA_V = 128
MLA_PAD = 256
Q_LORA = 768
KV_LORA = 512
MLA_ROPE_THETA = 10000.0
DIFF_HEADS = 8
DIFF_HD = 128
DIFF_V = 2 * DIFF_HD
ROT_DIM = DIFF_HD // 4
ROPE_THETA = 500000.0
PEER_HEADS = 8
N_KEYS = 128
N_EXPERTS = N_KEYS * N_KEYS
PEER_TOPK = 16
EPS = 1e-6

DV_W = DIFF_HEADS * DIFF_V
DQ_W = DIFF_HEADS * 2 * DIFF_HD
OFF_DV = 0
OFF_QA = OFF_DV + DV_W
OFF_KVA = OFF_QA + Q_LORA
OFF_KR = OFF_KVA + KV_LORA
OFF_DQ = OFF_KR + MLA_ROPE
OFF_DK = OFF_DQ + DQ_W
IN_WIDTH = OFF_DK + DQ_W

LANES = 128
VMEM_LIMIT_BYTES = 56 * 1024 * 1024
LOG2E = 1.4426950408889634
NEG_BIG = -1e30
POS_BIG = 1e30
PEER_TE = 512
NRM_ROWS = 8
NORM_SLACK = 1.01
MAX_SAFE_SPAN = 100.0


def _params(*sem, flags=None):
    return pltpu.CompilerParams(dimension_semantics=sem, vmem_limit_bytes=VMEM_LIMIT_BYTES, flags=flags)


def _lane_tile(g, width):
    reps = width // LANES
    return g if reps == 1 else jnp.concatenate([g] * reps, axis=1)


def _rms_rows(x, g, n):
    r = lax.rsqrt(jnp.sum(x * x, axis=0, keepdims=True) * (1.0 / n) + EPS)
    return x * r * g


def _norm_t_kernel(x_ref, g_ref, o_ref):
    x = x_ref[...]
    r = lax.rsqrt(jnp.sum(x * x, axis=0, keepdims=True) * (1.0 / x.shape[0]) + EPS)
    g = g_ref[...]
    for c in range(x.shape[1] // LANES):
        sl = slice(c * LANES, (c + 1) * LANES)
        o_ref[:, sl] = (x[:, sl] * r[:, sl] * g).astype(o_ref.dtype)


def _norm_t(x_t, g_rep, tt):
    d, t = x_t.shape
    return pl.pallas_call(
        _norm_t_kernel,
        grid=(t // tt,),
        in_specs=[pl.BlockSpec((d, tt), lambda i: (0, i)),
                  pl.BlockSpec((d, LANES), lambda i: (0, 0))],
        out_specs=pl.BlockSpec((d, tt), lambda i: (0, i)),
        out_shape=jax.ShapeDtypeStruct((d, t), BF16),
        compiler_params=_params("parallel"),
        name="norm_t",
    )(x_t, g_rep)


def _matmul_t_kernel(*refs, n_pairs, has_res):
    o_ref = refs[-1]
    acc = None
    for p in range(n_pairs):
        d = jnp.dot(refs[2 * p][...], refs[2 * p + 1][...], preferred_element_type=F32)
        acc = d if acc is None else acc + d
    if has_res:
        acc = refs[2 * n_pairs][...] + acc
    o_ref[...] = acc.astype(o_ref.dtype)


def _matmul_t(pairs, res, out_dtype, tt, tn, name):
    n = pairs[0][0].shape[0]
    t = pairs[0][1].shape[1]
    in_specs, args = [], []
    for w, a in pairs:
        in_specs.append(pl.BlockSpec((tn, w.shape[1]), lambda i, j: (j, 0)))
        in_specs.append(pl.BlockSpec((a.shape[0], tt), lambda i, j: (0, i)))
        args += [w, a]
    if res is not None:
        in_specs.append(pl.BlockSpec((tn, tt), lambda i, j: (j, i)))
        args.append(res)
    return pl.pallas_call(
        functools.partial(_matmul_t_kernel, n_pairs=len(pairs), has_res=res is not None),
        grid=(t // tt, n // tn),
        in_specs=in_specs,
        out_specs=pl.BlockSpec((tn, tt), lambda i, j: (j, i)),
        out_shape=jax.ShapeDtypeStruct((n, t), out_dtype),
        compiler_params=_params("parallel", "parallel"),
        name=name,
    )(*args)


def _rope_rows(x1, x2, cos, sin):
    return x1 * cos - x2 * sin, x2 * cos + x1 * sin


def _prep_kernel(y_ref, wq_ref, wkv_ref, gqa_ref, gkva_ref, gq_ref, gk_ref, gdq_ref, gdk_ref,
                 cm_ref, sm_ref, cd_ref, sd_ref,
                 q_ref, k_ref, v_ref, dq_ref, dk_ref, nrm_ref):
    tt = y_ref.shape[1]
    sumsq = lambda x: jnp.sum(x * x, axis=0, keepdims=True)
    q_ss = k_ss = dq_ss = dk_ss = jnp.zeros((1, tt), F32)
    cm, sm = cm_ref[...], sm_ref[...]
    cd, sd = cd_ref[...], sd_ref[...]
    half = MLA_ROPE // 2
    zpad = jnp.zeros((MLA_PAD - MLA_QK, tt), F32)
    q_scale = (MLA_QK ** -0.5) * LOG2E
    d_scale = (DIFF_HD ** -0.5) * LOG2E

    qa = y_ref[OFF_QA:OFF_QA + Q_LORA, :].astype(F32)
    qan = _rms_rows(qa, _lane_tile(gqa_ref[...], tt), Q_LORA).astype(BF16)
    q = jnp.dot(wq_ref[...], qan, preferred_element_type=F32)
    gq = _lane_tile(gq_ref[...], tt)
    for h in range(MLA_HEADS):
        qn = _rms_rows(q[h * MLA_QK:(h + 1) * MLA_QK], gq, MLA_QK)
        o1, o2 = _rope_rows(qn[MLA_NOPE:MLA_NOPE + half], qn[MLA_NOPE + half:], cm, sm)
        qfull = jnp.concatenate([qn[:MLA_NOPE], o1, o2, zpad], axis=0) * q_scale
        q_ss = jnp.maximum(q_ss, sumsq(qfull))
        q_ref[h] = qfull.astype(BF16)

    kva = y_ref[OFF_KVA:OFF_KVA + KV_LORA, :].astype(F32)
    kvan = _rms_rows(kva, _lane_tile(gkva_ref[...], tt), KV_LORA).astype(BF16)
    kv = jnp.dot(wkv_ref[...], kvan, preferred_element_type=F32)
    kr = y_ref[OFF_KR:OFF_KR + MLA_ROPE, :].astype(F32)
    kr_ss = jnp.sum(kr * kr, axis=0, keepdims=True)
    gk = _lane_tile(gk_ref[...], tt)
    krg = kr * gk[MLA_NOPE:]
    kr1, kr2 = _rope_rows(krg[:half], krg[half:], cm, sm)
    for h in range(MLA_HEADS):
        base = h * (MLA_NOPE + MLA_V)
        kn = kv[base:base + MLA_NOPE]
        r = lax.rsqrt((jnp.sum(kn * kn, axis=0, keepdims=True) + kr_ss) * (1.0 / MLA_QK) + EPS)
        kfull = jnp.concatenate([kn * r * gk[:MLA_NOPE], kr1 * r, kr2 * r, zpad], axis=0)
        k_ss = jnp.maximum(k_ss, sumsq(kfull))
        k_ref[h] = jnp.transpose(kfull).astype(BF16)
        v_ref[h] = kv[base + MLA_NOPE:base + MLA_NOPE + MLA_V].astype(BF16)

    gdq = _lane_tile(gdq_ref[...], tt)
    gdk = _lane_tile(gdk_ref[...], tt)
    hr = ROT_DIM // 2
    for j in range(2 * DIFF_HEADS):
        xq = _rms_rows(y_ref[OFF_DQ + j * DIFF_HD:OFF_DQ + (j + 1) * DIFF_HD, :].astype(F32), gdq, DIFF_HD)
        o1, o2 = _rope_rows(xq[:hr], xq[hr:ROT_DIM], cd, sd)
        dqfull = jnp.concatenate([o1, o2, xq[ROT_DIM:]], axis=0) * d_scale
        dq_ss = jnp.maximum(dq_ss, sumsq(dqfull))
        dq_ref[j] = dqfull.astype(BF16)
        xk = _rms_rows(y_ref[OFF_DK + j * DIFF_HD:OFF_DK + (j + 1) * DIFF_HD, :].astype(F32), gdk, DIFF_HD)
        o1, o2 = _rope_rows(xk[:hr], xk[hr:ROT_DIM], cd, sd)
        dkfull = jnp.concatenate([o1, o2, xk[ROT_DIM:]], axis=0)
        dk_ss = jnp.maximum(dk_ss, sumsq(dkfull))
        dk_ref[j] = jnp.transpose(dkfull).astype(BF16)
    row = lax.broadcasted_iota(jnp.int32, nrm_ref.shape, 0)
    nrm_ref[...] = jnp.where(row == 0, q_ss, jnp.where(row == 1, k_ss, jnp.where(row == 2, dq_ss, dk_ss)))


def _prep(y_t, wq_t, wkv_t, gains, tables, seq, tt):
    t = y_t.shape[1]
    nseq = seq // tt
    full = lambda a: pl.BlockSpec(a.shape, lambda i: (0,) * a.ndim)
    tab = lambda a: pl.BlockSpec((a.shape[0], tt), lambda i: (0, i % nseq))
    nh, nd = MLA_HEADS, 2 * DIFF_HEADS
    return pl.pallas_call(
        _prep_kernel,
        grid=(t // tt,),
        in_specs=[pl.BlockSpec((IN_WIDTH, tt), lambda i: (0, i)), full(wq_t), full(wkv_t)]
                 + [full(g) for g in gains] + [tab(a) for a in tables],
        out_specs=[pl.BlockSpec((nh, MLA_PAD, tt), lambda i: (0, 0, i)),
                   pl.BlockSpec((nh, tt, MLA_PAD), lambda i: (0, i, 0)),
                   pl.BlockSpec((nh, MLA_V, tt), lambda i: (0, 0, i)),
                   pl.BlockSpec((nd, DIFF_HD, tt), lambda i: (0, 0, i)),
                   pl.BlockSpec((nd, tt, DIFF_HD), lambda i: (0, i, 0)),
                   pl.BlockSpec((NRM_ROWS, tt), lambda i: (0, i))],
        out_shape=[jax.ShapeDtypeStruct((nh, MLA_PAD, t), BF16),
                   jax.ShapeDtypeStruct((nh, t, MLA_PAD), BF16),
                   jax.ShapeDtypeStruct((nh, MLA_V, t), BF16),
                   jax.ShapeDtypeStruct((nd, DIFF_HD, t), BF16),
                   jax.ShapeDtypeStruct((nd, t, DIFF_HD), BF16),
                   jax.ShapeDtypeStruct((NRM_ROWS, t), F32)],
        compiler_params=_params("parallel"),
        name="qkv_prep",
    )(y_t, wq_t, wkv_t, *gains, *tables)


class _Stream:
    def __init__(self, q, k_ref, bound, s_ref, p_ref, a_ref, m_ref, l_ref, acc_ref):
        self.q, self.k_ref, self.bound = q, k_ref, bound
        self.s_ref, self.p_ref, self.a_ref = s_ref, p_ref, a_ref
        self.m_ref, self.l_ref, self.acc_ref = m_ref, l_ref, acc_ref

    def init(self):
        self.m_ref[...] = jnp.full(self.m_ref.shape, NEG_BIG, F32) if self.bound is None else self.bound
        self.l_ref[...] = jnp.zeros(self.l_ref.shape, F32)
        self.acc_ref[...] = jnp.zeros(self.acc_ref.shape, F32)

    def qk(self, off, tk, slot):
        self.s_ref[slot] = jnp.dot(self.k_ref[0, pl.ds(off, tk), :], self.q, preferred_element_type=F32)

    def softmax(self, slot):
        s = self.s_ref[slot]
        if self.bound is not None:
            p = jnp.exp2(s - self.m_ref[...])
            self.l_ref[...] += jnp.sum(p, axis=0, keepdims=True)
            self.p_ref[slot] = p.astype(BF16)
            return
        m_prev = self.m_ref[...]
        m_new = jnp.maximum(m_prev, jnp.max(s, axis=0, keepdims=True))
        p = jnp.exp2(s - m_new)
        alpha = jnp.exp2(m_prev - m_new)
        self.l_ref[...] = alpha * self.l_ref[...] + jnp.sum(p, axis=0, keepdims=True)
        self.m_ref[...] = m_new
        self.a_ref[slot] = alpha
        self.p_ref[slot] = p.astype(BF16)

    def pv(self, vc, slot):
        pv = jnp.dot(vc, self.p_ref[slot], preferred_element_type=F32)
        if self.bound is not None:
            self.acc_ref[...] += pv
        else:
            self.acc_ref[...] = self.a_ref[slot] * self.acc_ref[...] + pv

    def result(self):
        return self.acc_ref[...] / self.l_ref[...]


def _attend(streams, v_chunk, n, tk, unroll):
    def offset(i):
        return i * tk if isinstance(i, int) else pl.multiple_of(i * tk, tk)

    def qk(i, slot):
        for st in streams:
            st.qk(offset(i), tk, slot)

    def pv(i, slot):
        vc = v_chunk(offset(i))
        for st in streams:
            st.pv(vc, slot)

    def softmax(slot):
        for st in streams:
            st.softmax(slot)

    for st in streams:
        st.init()
    qk(0, 0)
    qk(1, 1)
    softmax(0)

    def pair(j, carry):
        i = 2 * j + 1
        qk(i + 1, 0)
        pv(i - 1, 0)
        softmax(1)
        qk(i + 2, 1)
        pv(i, 1)
        softmax(0)
        return carry

    if unroll:
        for j in range((n - 2) // 2):
            pair(j, 0)
    else:
        lax.fori_loop(0, (n - 2) // 2, pair, 0)
    pv(n - 2, 0)
    softmax(1)
    pv(n - 1, 1)


def _stream_scratch(rows_v, tq, tk):
    return [pltpu.VMEM((2, tk, tq), F32), pltpu.VMEM((2, tk, tq), BF16), pltpu.VMEM((2, 1, tq), F32),
            pltpu.VMEM((1, tq), F32), pltpu.VMEM((1, tq), F32), pltpu.VMEM((rows_v, tq), F32)]


def _score_bound(qs, kb_ref, fast):
    if not fast:
        return None
    ss = None
    for q in qs:
        qf = q.astype(F32)
        v = jnp.sum(qf * qf, axis=0, keepdims=True)
        ss = v if ss is None else jnp.maximum(ss, v)
    return jnp.sqrt(ss) * _lane_tile(kb_ref[...], ss.shape[1])


def _mla_attn_kernel(q_ref, k_ref, v_ref, kb_ref, o_ref, *scratch, tk, fast):
    q = q_ref[0]
    st = _Stream(q, k_ref, _score_bound([q], kb_ref, fast), *scratch)
    _attend([st], lambda off: v_ref[0, :, pl.ds(off, tk)], k_ref.shape[1] // tk, tk, unroll=fast)
    o_ref[...] = st.result().astype(o_ref.dtype)


def _mla_attn(q_t, k, v_t, kb, batch, seq, tq, tk, fast):
    nh, _, t = q_t.shape
    nq = seq // tq
    return pl.pallas_call(
        functools.partial(_mla_attn_kernel, tk=tk, fast=fast),
        grid=(batch, nh, nq),
        in_specs=[pl.BlockSpec((1, MLA_PAD, tq), lambda b, h, i: (h, 0, b * nq + i)),
                  pl.BlockSpec((1, seq, MLA_PAD), lambda b, h, i: (h, b, 0)),
                  pl.BlockSpec((1, MLA_V, seq), lambda b, h, i: (h, 0, b)),
                  pl.BlockSpec(kb.shape, lambda b, h, i: (0, 0))],
        out_specs=pl.BlockSpec((MLA_V, tq), lambda b, h, i: (h, b * nq + i)),
        out_shape=jax.ShapeDtypeStruct((nh * MLA_V, t), BF16),
        scratch_shapes=_stream_scratch(MLA_V, tq, tk),
        compiler_params=_params("parallel", "parallel", "parallel"),
        name="mla_attn_fast" if fast else "mla_attn",
    )(q_t, k, v_t, kb)


def _diff_attn_kernel(q1_ref, q2_ref, k1_ref, k2_ref, v_ref, lam_ref, g_ref, kb_ref, o_ref, *scratch,
                      tk, lam_init, fast):
    q1, q2 = q1_ref[0], q2_ref[0]
    bound = _score_bound([q1, q2], kb_ref, fast)
    st1 = _Stream(q1, k1_ref, bound, *scratch[:6])
    st2 = _Stream(q2, k2_ref, bound, *scratch[6:])
    _attend([st1, st2], lambda off: v_ref[:, pl.ds(off, tk)], k1_ref.shape[1] // tk, tk, unroll=fast)
    lv = lam_ref[...]
    lam = (jnp.exp(jnp.sum(lv[0:1] * lv[1:2], axis=1, keepdims=True))
           - jnp.exp(jnp.sum(lv[2:3] * lv[3:4], axis=1, keepdims=True)) + lam_init)
    o = st1.result() - lam * st2.result()
    g = _lane_tile(g_ref[...], o.shape[1])
    o_ref[...] = (_rms_rows(o, g, DIFF_V) * (1.0 - lam_init)).astype(o_ref.dtype)


def _diff_attn(dq_t, dk, y_t, lam_vec, g_out, kb, batch, seq, tq, tk, lam_init, fast):
    t = dq_t.shape[2]
    nq = seq // tq
    return pl.pallas_call(
        functools.partial(_diff_attn_kernel, tk=tk, lam_init=lam_init, fast=fast),
        grid=(batch, DIFF_HEADS, nq),
        in_specs=[pl.BlockSpec((1, DIFF_HD, tq), lambda b, h, i: (2 * h, 0, b * nq + i)),
                  pl.BlockSpec((1, DIFF_HD, tq), lambda b, h, i: (2 * h + 1, 0, b * nq + i)),
                  pl.BlockSpec((1, seq, DIFF_HD), lambda b, h, i: (2 * h, b, 0)),
                  pl.BlockSpec((1, seq, DIFF_HD), lambda b, h, i: (2 * h + 1, b, 0)),
                  pl.BlockSpec((DIFF_V, seq), lambda b, h, i: (h, b)),
                  pl.BlockSpec(lam_vec.shape, lambda b, h, i: (0, 0)),
                  pl.BlockSpec(g_out.shape, lambda b, h, i: (0, 0)),
                  pl.BlockSpec(kb.shape, lambda b, h, i: (0, 0))],
        out_specs=pl.BlockSpec((DIFF_V, tq), lambda b, h, i: (h, b * nq + i)),
        out_shape=jax.ShapeDtypeStruct((DIFF_HEADS * DIFF_V, t), BF16),
        scratch_shapes=_stream_scratch(DIFF_V, tq, tk) + _stream_scratch(DIFF_V, tq, tk),
        compiler_params=_params("parallel", "parallel", "parallel"),
        name="diff_attn_fast" if fast else "diff_attn",
    )(dq_t, dq_t, dk, dk, y_t, lam_vec, g_out, kb)


def _top16_rows(a):
    vals = []
    for _ in range(PEER_TOPK):
        mx = jnp.max(a, axis=0, keepdims=True)
        vals.append(mx)
        a = jnp.where(a >= mx, NEG_BIG, a)
    return vals


def _peer_route_kernel(qp_ref, keys_ref, a1_ref, a2_ref, thr_ref):
    tt = qp_ref.shape[1]
    rows = lax.broadcasted_iota(jnp.int32, (PEER_TOPK, tt), 0)

    def head(h, carry):
        off = pl.multiple_of(h * 2 * N_KEYS, 2 * N_KEYS)
        sc1 = jnp.dot(keys_ref[h, 0], qp_ref[pl.ds(off, N_KEYS), :], preferred_element_type=F32)
        sc2 = jnp.dot(keys_ref[h, 1], qp_ref[pl.ds(off + N_KEYS, N_KEYS), :], preferred_element_type=F32)
        a1 = (sc1 - jnp.max(sc1, axis=0, keepdims=True)) * LOG2E
        a2 = (sc2 - jnp.max(sc2, axis=0, keepdims=True)) * LOG2E
        v1 = _top16_rows(a1)
        v2 = _top16_rows(a2)
        v2m = jnp.zeros((PEER_TOPK, tt), F32)
        for j in range(PEER_TOPK):
            v2m = jnp.where(rows == j, v2[j], v2m)
        cand = jnp.concatenate([v1[i] + v2m for i in range(PEER_TOPK)], axis=0)
        tops = _top16_rows(cand)
        z = tops[0] * 0.0
        for tv in tops:
            z = z + jnp.exp2(tv)
        lz = jnp.log2(z)
        cand_z = jnp.concatenate([(v1[i] - lz) + v2m for i in range(PEER_TOPK)], axis=0)
        thr = jnp.min(jnp.where(cand >= tops[-1], cand_z, POS_BIG), axis=0, keepdims=True)
        a1_ref[h] = a1 - lz
        a2_ref[h] = a2
        thr_ref[h] = thr
        return carry

    lax.fori_loop(0, PEER_HEADS, head, 0)


def _peer_route(qp_t, keys, tt):
    t = qp_t.shape[1]
    return pl.pallas_call(
        _peer_route_kernel,
        grid=(t // tt,),
        in_specs=[pl.BlockSpec((qp_t.shape[0], tt), lambda i: (0, i)),
                  pl.BlockSpec(keys.shape, lambda i: (0, 0, 0, 0))],
        out_specs=[pl.BlockSpec((PEER_HEADS, N_KEYS, tt), lambda i: (0, 0, i)),
                   pl.BlockSpec((PEER_HEADS, N_KEYS, tt), lambda i: (0, 0, i)),
                   pl.BlockSpec((PEER_HEADS, 1, tt), lambda i: (0, 0, i))],
        out_shape=[jax.ShapeDtypeStruct((PEER_HEADS, N_KEYS, t), F32),
                   jax.ShapeDtypeStruct((PEER_HEADS, N_KEYS, t), F32),
                   jax.ShapeDtypeStruct((PEER_HEADS, 1, t), F32)],
        compiler_params=_params("parallel"),
        name="peer_route",
    )(qp_t, keys)


def _peer_dense_kernel(x_ref, u_ref, vt_ref, a1_ref, a2_ref, thr_ref, res_hbm, o_ref,
                       w0_ref, w1_ref, sem, *, te, n_tiles):
    i = pl.program_id(0)
    s = pl.program_id(1)
    tt = x_ref.shape[1]
    n_e = te // N_KEYS
    rows = o_ref.shape[0] // n_e
    tile = jnp.minimum(s, n_tiles - 1)

    def step(w_dst, w_src):
        for e in range(n_e):
            es = slice(e * N_KEYS, (e + 1) * N_KEYS)
            rs = slice(e * rows, (e + 1) * rows)
            if w_dst is not None:
                hh = jnp.dot(u_ref[es, :], x_ref[...], preferred_element_type=F32)
            if w_src is not None:
                o_ref[rs, :] += jnp.dot(vt_ref[0, rs, :], w_src[...], preferred_element_type=F32)
            if w_dst is not None:
                e1 = tile * n_e + e
                gate = jnp.zeros((N_KEYS, tt), F32)
                for h in range(PEER_HEADS):
                    sc = a2_ref[h] + a1_ref[h, pl.ds(e1, 1), :]
                    gate = gate + jnp.where(sc >= thr_ref[h], jnp.exp2(sc), 0.0)
                act = 0.5 * hh * (1.0 + lax.erf(hh * (2.0 ** -0.5)))
                w_dst[es, :] = (gate * act).astype(BF16)

    @pl.when(s == 0)
    def _():
        res = pltpu.make_async_copy(res_hbm.at[:, pl.ds(pl.multiple_of(i * tt, tt), tt)], o_ref, sem)
        res.start()
        step(w0_ref, None)
        res.wait()

    inner = jnp.logical_and(s > 0, s < n_tiles)
    pl.when(jnp.logical_and(inner, s % 2 == 0))(functools.partial(step, w0_ref, w1_ref))
    pl.when(jnp.logical_and(inner, s % 2 == 1))(functools.partial(step, w1_ref, w0_ref))
    pl.when(s == n_tiles)(functools.partial(step, None, (w0_ref, w1_ref)[(n_tiles - 1) % 2]))


def _peer_dense(xn_t, u, v_t, a1, a2, thr, res_t, tt, te):
    d, t = xn_t.shape
    n_tiles = N_EXPERTS // te
    once = dict(pipeline_mode=pl.Buffered(1))
    return pl.pallas_call(
        functools.partial(_peer_dense_kernel, te=te, n_tiles=n_tiles),
        grid=(t // tt, n_tiles + 1),
        in_specs=[pl.BlockSpec((d, tt), lambda i, s: (0, i), **once),
                  pl.BlockSpec((te, d), lambda i, s: (jnp.minimum(s, n_tiles - 1), 0)),
                  pl.BlockSpec((1, d, te), lambda i, s: (jnp.maximum(s - 1, 0), 0, 0)),
                  pl.BlockSpec((PEER_HEADS, N_KEYS, tt), lambda i, s: (0, 0, i), **once),
                  pl.BlockSpec((PEER_HEADS, N_KEYS, tt), lambda i, s: (0, 0, i), **once),
                  pl.BlockSpec((PEER_HEADS, 1, tt), lambda i, s: (0, 0, i), **once),
                  pl.BlockSpec(memory_space=pl.ANY)],
        out_specs=pl.BlockSpec((d, tt), lambda i, s: (0, i), **once),
        out_shape=jax.ShapeDtypeStruct((d, t), F32),
        scratch_shapes=[pltpu.VMEM((te, tt), BF16), pltpu.VMEM((te, tt), BF16), pltpu.SemaphoreType.DMA(())],
        compiler_params=_params("parallel", "arbitrary"),
        name="peer_dense",
    )(xn_t, u, v_t, a1, a2, thr, res_t)


def _rep(g):
    return jnp.broadcast_to(g.astype(F32)[:, None], (g.shape[0], LANES))


def _rope_tables(seq, dim, theta):
    inv = theta ** (-jnp.arange(0, dim, 2, dtype=F32) / dim)
    ang = inv[:, None] * jnp.arange(seq, dtype=F32)[None, :]
    return jnp.cos(ang), jnp.sin(ang)


def _tile(n, pref):
    return pref if n % pref == 0 else n


def _trunk_t(x_t, batch, seq, p):
    t = x_t.shape[1]
    tt = _tile(t, 512)
    tm = _tile(t, 1024)
    tp = _tile(seq, 256)
    tq = _tile(seq, 512)
    tk = 512 if seq % 1024 == 0 else seq // 2
    cm, sm = _rope_tables(seq, MLA_ROPE, MLA_ROPE_THETA)
    cd, sd = _rope_tables(seq, ROT_DIM, ROPE_THETA)
    depth = p["w_in"].shape[0]
    for l in range(depth):
        w_in = p["w_in"][l]
        w_in_t = jnp.concatenate([w_in[:, IN_WIDTH - DV_W:], w_in[:, :IN_WIDTH - DV_W]], axis=1).T.astype(BF16)
        wq_t = p["w_q_b"][l].T.astype(BF16)
        wkv_t = p["w_kv_b"][l].T.astype(BF16)
        wo_t = p["w_out"][l].T.astype(BF16)
        wpq_t = p["w_peer_q"][l].T.astype(BF16)
        keys = p["peer_keys"][l].astype(BF16)
        u = p["peer_u"][l].astype(BF16)
        v_t = jnp.swapaxes(p["peer_v"][l].reshape(N_EXPERTS // PEER_TE, PEER_TE, -1), 1, 2).astype(BF16)
        gains = [_rep(p[k][l]) for k in ("q_a_norm", "kv_a_norm", "mla_q_norm", "mla_k_norm",
                                         "diff_q_norm", "diff_k_norm")]
        lam_vec = jnp.stack([p[k][l].astype(F32) for k in ("lambda_q1", "lambda_k1", "lambda_q2", "lambda_k2")])
        lam_init = 0.8 - 0.6 * math.exp(-0.3 * l)

        xn = _norm_t(x_t, _rep(p["attn_norm"][l]), tt)
        y_t = _matmul_t([(w_in_t, xn)], None, BF16, tm, 576, "in_proj")
        q_t, k, vm_t, dq_t, dk, nrm = _prep(y_t, wq_t, wkv_t, gains, (cm, sm, cd, sd), seq, tp)
        qn, kn, dqn, dkn = [jnp.sqrt(jnp.max(nrm[r])) * NORM_SLACK for r in range(4)]
        kb_mla = jnp.full((1, LANES), kn, F32)
        kb_diff = jnp.full((1, LANES), dkn, F32)
        g_out = _rep(p["diff_out_norm"][l])
        o_mla = lax.cond(2.0 * qn * kn <= MAX_SAFE_SPAN,
                         lambda: _mla_attn(q_t, k, vm_t, kb_mla, batch, seq, tq, tk, True),
                         lambda: _mla_attn(q_t, k, vm_t, kb_mla, batch, seq, tq, tk, False))
        o_diff = lax.cond(2.0 * dqn * dkn <= MAX_SAFE_SPAN,
                          lambda: _diff_attn(dq_t, dk, y_t, lam_vec, g_out, kb_diff, batch, seq, tq, tk, lam_init, True),
                          lambda: _diff_attn(dq_t, dk, y_t, lam_vec, g_out, kb_diff, batch, seq, tq, tk, lam_init, False))
        nm = MLA_HEADS * MLA_V
        x_t = _matmul_t([(wo_t[:, :nm], o_mla), (wo_t[:, nm:], o_diff)], x_t, F32, tm, 512, "out_proj")

        xn2 = _norm_t(x_t, _rep(p["ffn_norm"][l]), tt)
        qp_t = _matmul_t([(wpq_t, xn2)], None, BF16, tm, 512, "peer_query")
        a1, a2, thr = _peer_route(qp_t, keys, _tile(t, 256))
        x_t = _peer_dense(xn2, u, v_t, a1, a2, thr, x_t, tm, PEER_TE)
    return x_t


def kernel(x_prompt, x_sample, attn_norm, w_in, q_a_norm, w_q_b, kv_a_norm, w_kv_b, mla_q_norm, mla_k_norm,
           diff_q_norm, diff_k_norm, lambda_q1, lambda_k1, lambda_q2, lambda_k2, diff_out_norm, w_out,
           ffn_norm, w_peer_q, peer_keys, peer_u, peer_v):
    p = dict(attn_norm=attn_norm, w_in=w_in, q_a_norm=q_a_norm, w_q_b=w_q_b, kv_a_norm=kv_a_norm,
             w_kv_b=w_kv_b, mla_q_norm=mla_q_norm, mla_k_norm=mla_k_norm, diff_q_norm=diff_q_norm,
             diff_k_norm=diff_k_norm, lambda_q1=lambda_q1, lambda_k1=lambda_k1, lambda_q2=lambda_q2,
             lambda_k2=lambda_k2, diff_out_norm=diff_out_norm, w_out=w_out, ffn_norm=ffn_norm,
             w_peer_q=w_peer_q, peer_keys=peer_keys, peer_u=peer_u, peer_v=peer_v)
    bp, seq, d = x_prompt.shape
    bs = x_sample.shape[0]
    assert x_sample.shape[1] == seq
    x = jnp.concatenate([x_prompt.reshape(bp * seq, d), x_sample.reshape(bs * seq, d)], axis=0)
    y = _trunk_t(x.T, bp + bs, seq, p).T
    return y[:bp * seq].reshape(bp, seq, d), y[bp * seq:].reshape(bs, seq, d)
```

```python
import functools
import math

import jax
import jax.numpy as jnp
from jax import lax
from jax.experimental import pallas as pl
from jax.experimental.pallas import tpu as pltpu

F32 = jnp.float32
BF16 = jnp.bfloat16

D_MODEL = 4096
MLA_HEADS = 16
MLA_NOPE = 128
MLA_ROPE = 64
MLA_QK = MLA_NOPE + MLA_ROPE
MLA_V = 128
MLA_PAD = 256
Q_LORA = 768
KV_LORA = 512
MLA_ROPE_THETA = 10000.0
DIFF_HEADS = 8
DIFF_HD = 128
DIFF_V = 2 * DIFF_HD
ROT_DIM = DIFF_HD // 4
ROPE_THETA = 500000.0
PEER_HEADS = 8
N_KEYS = 128
N_EXPERTS = N_KEYS * N_KEYS
PEER_TOPK = 16
EPS = 1e-6

DV_W = DIFF_HEADS * DIFF_V
DQ_W = DIFF_HEADS * 2 * DIFF_HD
OFF_DV = 0
OFF_QA = OFF_DV + DV_W
OFF_KVA = OFF_QA + Q_LORA
OFF_KR = OFF_KVA + KV_LORA
OFF_DQ = OFF_KR + MLA_ROPE
OFF_DK = OFF_DQ + DQ_W
IN_WIDTH = OFF_DK + DQ_W

LANES = 128
VMEM_LIMIT_BYTES = 56 * 1024 * 1024
LOG2E = 1.4426950408889634
NEG_BIG = -1e30
POS_BIG = 1e30
PEER_TE = 512
NRM_ROWS = 8
NORM_SLACK = 1.01
MAX_SAFE_SPAN = 100.0


def _params(*sem, flags=None):
    return pltpu.CompilerParams(dimension_semantics=sem, vmem_limit_bytes=VMEM_LIMIT_BYTES, flags=flags)


def _lane_tile(g, width):
    reps = width // LANES
    return g if reps == 1 else jnp.concatenate([g] * reps, axis=1)


def _rms_rows(x, g, n):
    r = lax.rsqrt(jnp.sum(x * x, axis=0, keepdims=True) * (1.0 / n) + EPS)
    return x * r * g


def _norm_t_kernel(x_ref, g_ref, o_ref):
    x = x_ref[...]
    r = lax.rsqrt(jnp.sum(x * x, axis=0, keepdims=True) * (1.0 / x.shape[0]) + EPS)
    g = g_ref[...]
    for c in range(x.shape[1] // LANES):
        sl = slice(c * LANES, (c + 1) * LANES)
        o_ref[:, sl] = (x[:, sl] * r[:, sl] * g).astype(o_ref.dtype)


def _norm_t(x_t, g_rep, tt):
    d, t = x_t.shape
    return pl.pallas_call(
        _norm_t_kernel,
        grid=(t // tt,),
        in_specs=[pl.BlockSpec((d, tt), lambda i: (0, i)),
                  pl.BlockSpec((d, LANES), lambda i: (0, 0))],
        out_specs=pl.BlockSpec((d, tt), lambda i: (0, i)),
        out_shape=jax.ShapeDtypeStruct((d, t), BF16),
        compiler_params=_params("parallel"),
        name="norm_t",
    )(x_t, g_rep)


def _matmul_t_kernel(*refs, n_pairs, has_res):
    o_ref = refs[-1]
    acc = None
    for p in range(n_pairs):
        d = jnp.dot(refs[2 * p][...], refs[2 * p + 1][...], preferred_element_type=F32)
        acc = d if acc is None else acc + d
    if has_res:
        acc = refs[2 * n_pairs][...] + acc
    o_ref[...] = acc.astype(o_ref.dtype)


def _matmul_t(pairs, res, out_dtype, tt, tn, name):
    n = pairs[0][0].shape[0]
    t = pairs[0][1].shape[1]
    in_specs, args = [], []
    for w, a in pairs:
        in_specs.append(pl.BlockSpec((tn, w.shape[1]), lambda i, j: (j, 0)))
        in_specs.append(pl.BlockSpec((a.shape[0], tt), lambda i, j: (0, i)))
        args += [w, a]
    if res is not None:
        in_specs.append(pl.BlockSpec((tn, tt), lambda i, j: (j, i)))
        args.append(res)
    return pl.pallas_call(
        functools.partial(_matmul_t_kernel, n_pairs=len(pairs), has_res=res is not None),
        grid=(t // tt, n // tn),
        in_specs=in_specs,
        out_specs=pl.BlockSpec((tn, tt), lambda i, j: (j, i)),
        out_shape=jax.ShapeDtypeStruct((n, t), out_dtype),
        compiler_params=_params("parallel", "parallel"),
        name=name,
    )(*args)


def _rope_rows(x1, x2, cos, sin):
    return x1 * cos - x2 * sin, x2 * cos + x1 * sin


def _prep_kernel(y_ref, wq_ref, wkv_ref, gqa_ref, gkva_ref, gq_ref, gk_ref, gdq_ref, gdk_ref,
                 cm_ref, sm_ref, cd_ref, sd_ref,
                 q_ref, k_ref, v_ref, dq_ref, dk_ref, nrm_ref):
    tt = y_ref.shape[1]
    sumsq = lambda x: jnp.sum(x * x, axis=0, keepdims=True)
    q_ss = k_ss = dq_ss = dk_ss = jnp.zeros((1, tt), F32)
    cm, sm = cm_ref[...], sm_ref[...]
    cd, sd = cd_ref[...], sd_ref[...]
    half = MLA_ROPE // 2
    zpad = jnp.zeros((MLA_PAD - MLA_QK, tt), F32)
    q_scale = (MLA_QK ** -0.5) * LOG2E
    d_scale = (DIFF_HD ** -0.5) * LOG2E

    qa = y_ref[OFF_QA:OFF_QA + Q_LORA, :].astype(F32)
    qan = _rms_rows(qa, _lane_tile(gqa_ref[...], tt), Q_LORA).astype(BF16)
    q = jnp.dot(wq_ref[...], qan, preferred_element_type=F32)
    gq = _lane_tile(gq_ref[...], tt)
    for h in range(MLA_HEADS):
        qn = _rms_rows(q[h * MLA_QK:(h + 1) * MLA_QK], gq, MLA_QK)
        o1, o2 = _rope_rows(qn[MLA_NOPE:MLA_NOPE + half], qn[MLA_NOPE + half:], cm, sm)
        qfull = jnp.concatenate([qn[:MLA_NOPE], o1, o2, zpad], axis=0) * q_scale
        q_ss = jnp.maximum(q_ss, sumsq(qfull))
        q_ref[h] = qfull.astype(BF16)

    kva = y_ref[OFF_KVA:OFF_KVA + KV_LORA, :].astype(F32)
    kvan = _rms_rows(kva, _lane_tile(gkva_ref[...], tt), KV_LORA).astype(BF16)
    kv = jnp.dot(wkv_ref[...], kvan, preferred_element_type=F32)
    kr = y_ref[OFF_KR:OFF_KR + MLA_ROPE, :].astype(F32)
    kr_ss = jnp.sum(kr * kr, axis=0, keepdims=True)
    gk = _lane_tile(gk_ref[...], tt)
    krg = kr * gk[MLA_NOPE:]
    kr1, kr2 = _rope_rows(krg[:half], krg[half:], cm, sm)
    for h in range(MLA_HEADS):
        base = h * (MLA_NOPE + MLA_V)
        kn = kv[base:base + MLA_NOPE]
        r = lax.rsqrt((jnp.sum(kn * kn, axis=0, keepdims=True) + kr_ss) * (1.0 / MLA_QK) + EPS)
        kfull = jnp.concatenate([kn * r * gk[:MLA_NOPE], kr1 * r, kr2 * r, zpad], axis=0)
        k_ss = jnp.maximum(k_ss, sumsq(kfull))
        k_ref[h] = jnp.transpose(kfull).astype(BF16)
        v_ref[h] = kv[base + MLA_NOPE:base + MLA_NOPE + MLA_V].astype(BF16)

    gdq = _lane_tile(gdq_ref[...], tt)
    gdk = _lane_tile(gdk_ref[...], tt)
    hr = ROT_DIM // 2
    for j in range(2 * DIFF_HEADS):
        xq = _rms_rows(y_ref[OFF_DQ + j * DIFF_HD:OFF_DQ + (j + 1) * DIFF_HD, :].astype(F32), gdq, DIFF_HD)
        o1, o2 = _rope_rows(xq[:hr], xq[hr:ROT_DIM], cd, sd)
        dqfull = jnp.concatenate([o1, o2, xq[ROT_DIM:]], axis=0) * d_scale
        dq_ss = jnp.maximum(dq_ss, sumsq(dqfull))
        dq_ref[j] = dqfull.astype(BF16)
        xk = _rms_rows(y_ref[OFF_DK + j * DIFF_HD:OFF_DK + (j + 1) * DIFF_HD, :].astype(F32), gdk, DIFF_HD)
        o1, o2 = _rope_rows(xk[:hr], xk[hr:ROT_DIM], cd, sd)
        dkfull = jnp.concatenate([o1, o2, xk[ROT_DIM:]], axis=0)
        dk_ss = jnp.maximum(dk_ss, sumsq(dkfull))
        dk_ref[j] = jnp.transpose(dkfull).astype(BF16)
    row = lax.broadcasted_iota(jnp.int32, nrm_ref.shape, 0)
    nrm_ref[...] = jnp.where(row == 0, q_ss, jnp.where(row == 1, k_ss, jnp.where(row == 2, dq_ss, dk_ss)))


def _prep(y_t, wq_t, wkv_t, gains, tables, seq, tt):
    t = y_t.shape[1]
    nseq = seq // tt
    full = lambda a: pl.BlockSpec(a.shape, lambda i: (0,) * a.ndim)
    tab = lambda a: pl.BlockSpec((a.shape[0], tt), lambda i: (0, i % nseq))
    nh, nd = MLA_HEADS, 2 * DIFF_HEADS
    return pl.pallas_call(
        _prep_kernel,
        grid=(t // tt,),
        in_specs=[pl.BlockSpec((IN_WIDTH, tt), lambda i: (0, i)), full(wq_t), full(wkv_t)]
                 + [full(g) for g in gains] + [tab(a) for a in tables],
        out_specs=[pl.BlockSpec((nh, MLA_PAD, tt), lambda i: (0, 0, i)),
                   pl.BlockSpec((nh, tt, MLA_PAD), lambda i: (0, i, 0)),
                   pl.BlockSpec((nh, MLA_V, tt), lambda i: (0, 0, i)),
                   pl.BlockSpec((nd, DIFF_HD, tt), lambda i: (0, 0, i)),
                   pl.BlockSpec((nd, tt, DIFF_HD), lambda i: (0, i, 0)),
                   pl.BlockSpec((NRM_ROWS, tt), lambda i: (0, i))],
        out_shape=[jax.ShapeDtypeStruct((nh, MLA_PAD, t), BF16),
                   jax.ShapeDtypeStruct((nh, t, MLA_PAD), BF16),
                   jax.ShapeDtypeStruct((nh, MLA_V, t), BF16),
                   jax.ShapeDtypeStruct((nd, DIFF_HD, t), BF16),
                   jax.ShapeDtypeStruct((nd, t, DIFF_HD), BF16),
                   jax.ShapeDtypeStruct((NRM_ROWS, t), F32)],
        compiler_params=_params("parallel"),
        name="qkv_prep",
    )(y_t, wq_t, wkv_t, *gains, *tables)


class _Stream:
    def __init__(self, q, k_ref, bound, s_ref, p_ref, a_ref, m_ref, l_ref, acc_ref):
        self.q, self.k_ref, self.bound = q, k_ref, bound
        self.s_ref, self.p_ref, self.a_ref = s_ref, p_ref, a_ref
        self.m_ref, self.l_ref, self.acc_ref = m_ref, l_ref, acc_ref

    def init(self):
        self.m_ref[...] = jnp.full(self.m_ref.shape, NEG_BIG, F32) if self.bound is None else self.bound
        self.l_ref[...] = jnp.zeros(self.l_ref.shape, F32)
        self.acc_ref[...] = jnp.zeros(self.acc_ref.shape, F32)

    def qk(self, off, tk, slot):
        self.s_ref[slot] = jnp.dot(self.k_ref[0, pl.ds(off, tk), :], self.q, preferred_element_type=F32)

    def softmax(self, slot):
        s = self.s_ref[slot]
        if self.bound is not None:
            p = jnp.exp2(s - self.m_ref[...])
            self.l_ref[...] += jnp.sum(p, axis=0, keepdims=True)
            self.p_ref[slot] = p.astype(BF16)
            return
        m_prev = self.m_ref[...]
        m_new = jnp.maximum(m_prev, jnp.max(s, axis=0, keepdims=True))
        p = jnp.exp2(s - m_new)
        alpha = jnp.exp2(m_prev - m_new)
        self.l_ref[...] = alpha * self.l_ref[...] + jnp.sum(p, axis=0, keepdims=True)
        self.m_ref[...] = m_new
        self.a_ref[slot] = alpha
        self.p_ref[slot] = p.astype(BF16)

    def pv(self, vc, slot):
        pv = jnp.dot(vc, self.p_ref[slot], preferred_element_type=F32)
        if self.bound is not None:
            self.acc_ref[...] += pv
        else:
            self.acc_ref[...] = self.a_ref[slot] * self.acc_ref[...] + pv

    def result(self):
        return self.acc_ref[...] / self.l_ref[...]


def _attend(streams, v_chunk, n, tk, unroll):
    def offset(i):
        return i * tk if isinstance(i, int) else pl.multiple_of(i * tk, tk)

    def qk(i, slot):
        for st in streams:
            st.qk(offset(i), tk, slot)

    def pv(i, slot):
        vc = v_chunk(offset(i))
        for st in streams:
            st.pv(vc, slot)

    def softmax(slot):
        for st in streams:
            st.softmax(slot)

    for st in streams:
        st.init()
    qk(0, 0)
    qk(1, 1)
    softmax(0)

    def pair(j, carry):
        i = 2 * j + 1
        qk(i + 1, 0)
        pv(i - 1, 0)
        softmax(1)
        qk(i + 2, 1)
        pv(i, 1)
        softmax(0)
        return carry

    if unroll:
        for j in range((n - 2) // 2):
            pair(j, 0)
    else:
        lax.fori_loop(0, (n - 2) // 2, pair, 0)
    pv(n - 2, 0)
    softmax(1)
    pv(n - 1, 1)


def _stream_scratch(rows_v, tq, tk):
    return [pltpu.VMEM((2, tk, tq), F32), pltpu.VMEM((2, tk, tq), BF16), pltpu.VMEM((2, 1, tq), F32),
            pltpu.VMEM((1, tq), F32), pltpu.VMEM((1, tq), F32), pltpu.VMEM((rows_v, tq), F32)]


def _score_bound(qs, kb_ref, fast):
    if not fast:
        return None
    ss = None
    for q in qs:
        qf = q.astype(F32)
        v = jnp.sum(qf * qf, axis=0, keepdims=True)
        ss = v if ss is None else jnp.maximum(ss, v)
    return jnp.sqrt(ss) * _lane_tile(kb_ref[...], ss.shape[1])


def _mla_attn_kernel(q_ref, k_ref, v_ref, kb_ref, o_ref, *scratch, tk, fast):
    q = q_ref[0]
    st = _Stream(q, k_ref, _score_bound([q], kb_ref, fast), *scratch)
    _attend([st], lambda off: v_ref[0, :, pl.ds(off, tk)], k_ref.shape[1] // tk, tk, unroll=fast)
    o_ref[...] = st.result().astype(o_ref.dtype)


def _mla_attn(q_t, k, v_t, kb, batch, seq, tq, tk, fast):
    nh, _, t = q_t.shape
    nq = seq // tq
    return pl.pallas_call(
        functools.partial(_mla_attn_kernel, tk=tk, fast=fast),
        grid=(batch, nh, nq),
        in_specs=[pl.BlockSpec((1, MLA_PAD, tq), lambda b, h, i: (h, 0, b * nq + i)),
                  pl.BlockSpec((1, seq, MLA_PAD), lambda b, h, i: (h, b, 0)),
                  pl.BlockSpec((1, MLA_V, seq), lambda b, h, i: (h, 0, b)),
                  pl.BlockSpec(kb.shape, lambda b, h, i: (0, 0))],
        out_specs=pl.BlockSpec((MLA_V, tq), lambda b, h, i: (h, b * nq + i)),
        out_shape=jax.ShapeDtypeStruct((nh * MLA_V, t), BF16),
        scratch_shapes=_stream_scratch(MLA_V, tq, tk),
        compiler_params=_params("parallel", "parallel", "parallel"),
        name="mla_attn_fast" if fast else "mla_attn",
    )(q_t, k, v_t, kb)


def _diff_attn_kernel(q1_ref, q2_ref, k1_ref, k2_ref, v_ref, lam_ref, g_ref, kb_ref, o_ref, *scratch,
                      tk, lam_init, fast):
    q1, q2 = q1_ref[0], q2_ref[0]
    bound = _score_bound([q1, q2], kb_ref, fast)
    st1 = _Stream(q1, k1_ref, bound, *scratch[:6])
    st2 = _Stream(q2, k2_ref, bound, *scratch[6:])
    _attend([st1, st2], lambda off: v_ref[:, pl.ds(off, tk)], k1_ref.shape[1] // tk, tk, unroll=fast)
    lv = lam_ref[...]
    lam = (jnp.exp(jnp.sum(lv[0:1] * lv[1:2], axis=1, keepdims=True))
           - jnp.exp(jnp.sum(lv[2:3] * lv[3:4], axis=1, keepdims=True)) + lam_init)
    o = st1.result() - lam * st2.result()
    g = _lane_tile(g_ref[...], o.shape[1])
    o_ref[...] = (_rms_rows(o, g, DIFF_V) * (1.0 - lam_init)).astype(o_ref.dtype)


def _diff_attn(dq_t, dk, y_t, lam_vec, g_out, kb, batch, seq, tq, tk, lam_init, fast):
    t = dq_t.shape[2]
    nq = seq // tq
    return pl.pallas_call(
        functools.partial(_diff_attn_kernel, tk=tk, lam_init=lam_init, fast=fast),
        grid=(batch, DIFF_HEADS, nq),
        in_specs=[pl.BlockSpec((1, DIFF_HD, tq), lambda b, h, i: (2 * h, 0, b * nq + i)),
                  pl.BlockSpec((1, DIFF_HD, tq), lambda b, h, i: (2 * h + 1, 0, b * nq + i)),
                  pl.BlockSpec((1, seq, DIFF_HD), lambda b, h, i: (2 * h, b, 0)),
                  pl.BlockSpec((1, seq, DIFF_HD), lambda b, h, i: (2 * h + 1, b, 0)),
                  pl.BlockSpec((DIFF_V, seq), lambda b, h, i: (h, b)),
                  pl.BlockSpec(lam_vec.shape, lambda b, h, i: (0, 0)),
                  pl.BlockSpec(g_out.shape, lambda b, h, i: (0, 0)),
                  pl.BlockSpec(kb.shape, lambda b, h, i: (0, 0))],
        out_specs=pl.BlockSpec((DIFF_V, tq), lambda b, h, i: (h, b * nq + i)),
        out_shape=jax.ShapeDtypeStruct((DIFF_HEADS * DIFF_V, t), BF16),
        scratch_shapes=_stream_scratch(DIFF_V, tq, tk) + _stream_scratch(DIFF_V, tq, tk),
        compiler_params=_params("parallel", "parallel", "parallel"),
        name="diff_attn_fast" if fast else "diff_attn",
    )(dq_t, dq_t, dk, dk, y_t, lam_vec, g_out, kb)


def _top16_rows(a):
    vals = []
    for _ in range(PEER_TOPK):
        mx = jnp.max(a, axis=0, keepdims=True)
        vals.append(mx)
        a = jnp.where(a >= mx, NEG_BIG, a)
    return vals


def _peer_route_kernel(qp_ref, keys_ref, a1_ref, a2_ref, thr_ref):
    tt = qp_ref.shape[1]
    rows = lax.broadcasted_iota(jnp.int32, (PEER_TOPK, tt), 0)

    def head(h, carry):
        off = pl.multiple_of(h * 2 * N_KEYS, 2 * N_KEYS)
        sc1 = jnp.dot(keys_ref[h, 0], qp_ref[pl.ds(off, N_KEYS), :], preferred_element_type=F32)
        sc2 = jnp.dot(keys_ref[h, 1], qp_ref[pl.ds(off + N_KEYS, N_KEYS), :], preferred_element_type=F32)
        a1 = (sc1 - jnp.max(sc1, axis=0, keepdims=True)) * LOG2E
        a2 = (sc2 - jnp.max(sc2, axis=0, keepdims=True)) * LOG2E
        v1 = _top16_rows(a1)
        v2 = _top16_rows(a2)
        v2m = jnp.zeros((PEER_TOPK, tt), F32)
        for j in range(PEER_TOPK):
            v2m = jnp.where(rows == j, v2[j], v2m)
        cand = jnp.concatenate([v1[i] + v2m for i in range(PEER_TOPK)], axis=0)
        tops = _top16_rows(cand)
        z = tops[0] * 0.0
        for tv in tops:
            z = z + jnp.exp2(tv)
        lz = jnp.log2(z)
        cand_z = jnp.concatenate([(v1[i] - lz) + v2m for i in range(PEER_TOPK)], axis=0)
        thr = jnp.min(jnp.where(cand >= tops[-1], cand_z, POS_BIG), axis=0, keepdims=True)
        a1_ref[h] = a1 - lz
        a2_ref[h] = a2
        thr_ref[h] = thr
        return carry

    lax.fori_loop(0, PEER_HEADS, head, 0)


def _peer_route(qp_t, keys, tt):
    t = qp_t.shape[1]
    return pl.pallas_call(
        _peer_route_kernel,
        grid=(t // tt,),
        in_specs=[pl.BlockSpec((qp_t.shape[0], tt), lambda i: (0, i)),
                  pl.BlockSpec(keys.shape, lambda i: (0, 0, 0, 0))],
        out_specs=[pl.BlockSpec((PEER_HEADS, N_KEYS, tt), lambda i: (0, 0, i)),
                   pl.BlockSpec((PEER_HEADS, N_KEYS, tt), lambda i: (0, 0, i)),
                   pl.BlockSpec((PEER_HEADS, 1, tt), lambda i: (0, 0, i))],
        out_shape=[jax.ShapeDtypeStruct((PEER_HEADS, N_KEYS, t), F32),
                   jax.ShapeDtypeStruct((PEER_HEADS, N_KEYS, t), F32),
                   jax.ShapeDtypeStruct((PEER_HEADS, 1, t), F32)],
        compiler_params=_params("parallel"),
        name="peer_route",
    )(qp_t, keys)


def _peer_dense_kernel(x_ref, u_ref, vt_ref, a1_ref, a2_ref, thr_ref, res_hbm, o_ref,
                       w0_ref, w1_ref, sem, *, te, n_tiles):
    i = pl.program_id(0)
    s = pl.program_id(1)
    tt = x_ref.shape[1]
    n_e = te // N_KEYS
    rows = o_ref.shape[0] // n_e
    tile = jnp.minimum(s, n_tiles - 1)

    def step(w_dst, w_src):
        for e in range(n_e):
            es = slice(e * N_KEYS, (e + 1) * N_KEYS)
            rs = slice(e * rows, (e + 1) * rows)
            if w_dst is not None:
                hh = jnp.dot(u_ref[es, :], x_ref[...], preferred_element_type=F32)
            if w_src is not None:
                o_ref[rs, :] += jnp.dot(vt_ref[0, rs, :], w_src[...], preferred_element_type=F32)
            if w_dst is not None:
                e1 = tile * n_e + e
                gate = jnp.zeros((N_KEYS, tt), F32)
                for h in range(PEER_HEADS):
                    sc = a2_ref[h] + a1_ref[h, pl.ds(e1, 1), :]
                    gate = gate + jnp.where(sc >= thr_ref[h], jnp.exp2(sc), 0.0)
                act = 0.5 * hh * (1.0 + lax.erf(hh * (2.0 ** -0.5)))
                w_dst[es, :] = (gate * act).astype(BF16)

    @pl.when(s == 0)
    def _():
        res = pltpu.make_async_copy(res_hbm.at[:, pl.ds(pl.multiple_of(i * tt, tt), tt)], o_ref, sem)
        res.start()
        step(w0_ref, None)
        res.wait()

    inner = jnp.logical_and(s > 0, s < n_tiles)
    pl.when(jnp.logical_and(inner, s % 2 == 0))(functools.partial(step, w0_ref, w1_ref))
    pl.when(jnp.logical_and(inner, s % 2 == 1))(functools.partial(step, w1_ref, w0_ref))
    pl.when(s == n_tiles)(functools.partial(step, None, (w0_ref, w1_ref)[(n_tiles - 1) % 2]))


def _peer_dense(xn_t, u, v_t, a1, a2, thr, res_t, tt, te):
    d, t = xn_t.shape
    n_tiles = N_EXPERTS // te
    once = dict(pipeline_mode=pl.Buffered(1))
    return pl.pallas_call(
        functools.partial(_peer_dense_kernel, te=te, n_tiles=n_tiles),
        grid=(t // tt, n_tiles + 1),
        in_specs=[pl.BlockSpec((d, tt), lambda i, s: (0, i), **once),
                  pl.BlockSpec((te, d), lambda i, s: (jnp.minimum(s, n_tiles - 1), 0)),
                  pl.BlockSpec((1, d, te), lambda i, s: (jnp.maximum(s - 1, 0), 0, 0)),
                  pl.BlockSpec((PEER_HEADS, N_KEYS, tt), lambda i, s: (0, 0, i), **once),
                  pl.BlockSpec((PEER_HEADS, N_KEYS, tt), lambda i, s: (0, 0, i), **once),
                  pl.BlockSpec((PEER_HEADS, 1, tt), lambda i, s: (0, 0, i), **once),
                  pl.BlockSpec(memory_space=pl.ANY)],
        out_specs=pl.BlockSpec((d, tt), lambda i, s: (0, i)),
        out_shape=jax.ShapeDtypeStruct((d, t), F32),
        scratch_shapes=[pltpu.VMEM((te, tt), BF16), pltpu.VMEM((te, tt), BF16), pltpu.SemaphoreType.DMA(())],
        compiler_params=_params("parallel", "arbitrary"),
        name="peer_dense",
    )(xn_t, u, v_t, a1, a2, thr, res_t)


def _rep(g):
    return jnp.broadcast_to(g.astype(F32)[:, None], (g.shape[0], LANES))


def _rope_tables(seq, dim, theta):
    inv = theta ** (-jnp.arange(0, dim, 2, dtype=F32) / dim)
    ang = inv[:, None] * jnp.arange(seq, dtype=F32)[None, :]
    return jnp.cos(ang), jnp.sin(ang)


def _tile(n, pref):
    return pref if n % pref == 0 else n


def _trunk_t(x_t, batch, seq, p):
    t = x_t.shape[1]
    tt = _tile(t, 512)
    tm = _tile(t, 1024)
    tp = _tile(seq, 256)
    tq = _tile(seq, 512)
    tk = 512 if seq % 1024 == 0 else seq // 2
    cm, sm = _rope_tables(seq, MLA_ROPE, MLA_ROPE_THETA)
    cd, sd = _rope_tables(seq, ROT_DIM, ROPE_THETA)
    depth = p["w_in"].shape[0]
    for l in range(depth):
        w_in = p["w_in"][l]
        w_in_t = jnp.concatenate([w_in[:, IN_WIDTH - DV_W:], w_in[:, :IN_WIDTH - DV_W]], axis=1).T.astype(BF16)
        wq_t = p["w_q_b"][l].T.astype(BF16)
        wkv_t = p["w_kv_b"][l].T.astype(BF16)
        wo_t = p["w_out"][l].T.astype(BF16)
        wpq_t = p["w_peer_q"][l].T.astype(BF16)
        keys = p["peer_keys"][l].astype(BF16)
        u = p["peer_u"][l].astype(BF16)
        v_t = jnp.swapaxes(p["peer_v"][l].reshape(N_EXPERTS // PEER_TE, PEER_TE, -1), 1, 2).astype(BF16)
        gains = [_rep(p[k][l]) for k in ("q_a_norm", "kv_a_norm", "mla_q_norm", "mla_k_norm",
                                         "diff_q_norm", "diff_k_norm")]
        lam_vec = jnp.stack([p[k][l].astype(F32) for k in ("lambda_q1", "lambda_k1", "lambda_q2", "lambda_k2")])
        lam_init = 0.8 - 0.6 * math.exp(-0.3 * l)

        xn = _norm_t(x_t, _rep(p["attn_norm"][l]), tt)
        y_t = _matmul_t([(w_in_t, xn)], None, BF16, tm, 576, "in_proj")
        q_t, k, vm_t, dq_t, dk, nrm = _prep(y_t, wq_t, wkv_t, gains, (cm, sm, cd, sd), seq, tp)
        qn, kn, dqn, dkn = [jnp.sqrt(jnp.max(nrm[r])) * NORM_SLACK for r in range(4)]
        kb_mla = jnp.full((1, LANES), kn, F32)
        kb_diff = jnp.full((1, LANES), dkn, F32)
        g_out = _rep(p["diff_out_norm"][l])
        o_mla = lax.cond(2.0 * qn * kn <= MAX_SAFE_SPAN,
                         lambda: _mla_attn(q_t, k, vm_t, kb_mla, batch, seq, tq, tk, True),
                         lambda: _mla_attn(q_t, k, vm_t, kb_mla, batch, seq, tq, tk, False))
        o_diff = lax.cond(2.0 * dqn * dkn <= MAX_SAFE_SPAN,
                          lambda: _diff_attn(dq_t, dk, y_t, lam_vec, g_out, kb_diff, batch, seq, tq, tk, lam_init, True),
                          lambda: _diff_attn(dq_t, dk, y_t, lam_vec, g_out, kb_diff, batch, seq, tq, tk, lam_init, False))
        nm = MLA_HEADS * MLA_V
        x_t = _matmul_t([(wo_t[:, :nm], o_mla), (wo_t[:, nm:], o_diff)], x_t, F32, tm, 512, "out_proj")

        xn2 = _norm_t(x_t, _rep(p["ffn_norm"][l]), tt)
        qp_t = _matmul_t([(wpq_t, xn2)], None, BF16, tm, 512, "peer_query")
        a1, a2, thr = _peer_route(qp_t, keys, _tile(t, 256))
        x_t = _peer_dense(xn2, u, v_t, a1, a2, thr, x_t, tt, PEER_TE)
    return x_t


def kernel(x_prompt, x_sample, attn_norm, w_in, q_a_norm, w_q_b, kv_a_norm, w_kv_b, mla_q_norm, mla_k_norm,
           diff_q_norm, diff_k_norm, lambda_q1, lambda_k1, lambda_q2, lambda_k2, diff_out_norm, w_out,
           ffn_norm, w_peer_q, peer_keys, peer_u, peer_v):
    p = dict(attn_norm=attn_norm, w_in=w_in, q_a_norm=q_a_norm, w_q_b=w_q_b, kv_a_norm=kv_a_norm,
             w_kv_b=w_kv_b, mla_q_norm=mla_q_norm, mla_k_norm=mla_k_norm, diff_q_norm=diff_q_norm,
             diff_k_norm=diff_k_norm, lambda_q1=lambda_q1, lambda_k1=lambda_k1, lambda_q2=lambda_q2,
             lambda_k2=lambda_k2, diff_out_norm=diff_out_norm, w_out=w_out, ffn_norm=ffn_norm,
             w_peer_q=w_peer_q, peer_keys=peer_keys, peer_u=peer_u, peer_v=peer_v)
    bp, seq, d = x_prompt.shape
    bs = x_sample.shape[0]
    assert x_sample.shape[1] == seq
    x_t = jnp.concatenate([x_prompt.reshape(bp * seq, d).T, x_sample.reshape(bs * seq, d).T], axis=1)
    y_t = _trunk_t(x_t, bp + bs, seq, p)
    return y_t[:, :bp * seq].T.reshape(bp, seq, d), y_t[:, bp * seq:].T.reshape(bs, seq, d)
```
